```python
import math
import jax, jax.numpy as jnp
from jax import lax
import numpy as np

D_MODEL = 1024
BATCH = 4
SEQ = 8192
DEPTH = 2
DEC_BATCH = 32
DEC_SEQ = 64
PAST_LEN = 1024

CHUNK = 64
D_CONV = D_MODEL
CONV_W = 31
D_GMLP = D_MODEL
GMLP_CHUNK = 128
GMLP_GROUPS = 8
GMLP_GW = D_GMLP // GMLP_GROUPS
N_HEADS = 8
HEAD_DIM = 128
N_KV = 2
GQA_G = N_HEADS // N_KV
IDX_HEADS = 8
IDX_DIM = 64
TOPK_MAX = 256
Q_BLOCK = 128
D_FF = 4 * D_MODEL
N_BRANCH = 3
EPS = 1e-6
IN_SPLIT = (2 * D_CONV, 2 * D_GMLP, N_HEADS * HEAD_DIM, N_KV * HEAD_DIM, N_KV * HEAD_DIM,
            IDX_HEADS * IDX_DIM, IDX_DIM, IDX_HEADS, N_BRANCH * D_MODEL)
IN_COLS = 2 * D_CONV + 2 * D_GMLP + N_HEADS * HEAD_DIM + 2 * N_KV * HEAD_DIM + IDX_HEADS * IDX_DIM + IDX_DIM + IDX_HEADS + N_BRANCH * D_MODEL

kernel_name = 'hybrid_stream_conv_gmlp_dsa_step'


def rms_norm(x, g):
    xf = x.astype(jnp.float32)
    y = xf * lax.rsqrt(jnp.mean(xf * xf, axis=-1, keepdims=True) + EPS)
    return (y * g).astype(x.dtype)


def layer_norm(x, g, b):
    xf = x.astype(jnp.float32)
    mu = jnp.mean(xf, axis=-1, keepdims=True)
    var = jnp.mean(jnp.square(xf - mu), axis=-1, keepdims=True)
    return ((xf - mu) * lax.rsqrt(var + EPS) * g + b).astype(x.dtype)


def split_cols(p):
    out, start = [], 0
    for n in IN_SPLIT:
        out.append(p[..., start:start + n])
        start += n
    return out


def causal_dwconv(xpad, w, b):
    y = lax.conv_general_dilated(xpad, w[:, None, :], window_strides=(1,), padding='VALID',
                                 dimension_numbers=('NWC', 'WIO', 'NWC'),
                                 feature_group_count=xpad.shape[-1])
    return y + b


def spatial_gate(v, w_s, b_s, L):
    B, T, _ = v.shape
    vb = v.reshape(B, T // L, L, GMLP_GROUPS, GMLP_GW)
    mask = jnp.tril(jnp.ones((L, L), dtype=bool))
    wm = jnp.where(mask[None], w_s[:, :L, :L], 0)
    s = jnp.einsum('gij,bcjge->bcige', wm, vb)
    s = s + jnp.swapaxes(b_s[:, :L], 0, 1)[None, None, :, :, None]
    return s.reshape(B, T, D_GMLP)


def sparse_attention(q, k, v, qi, ki, wi, q_pos, topk, block):
    B, T = q.shape[0], q.shape[1]
    S = k.shape[1]
    nb = T // block
    k_chunk = jnp.arange(S, dtype=jnp.int32) // CHUNK
    slopes = jnp.exp2(-8.0 * jnp.arange(1, N_HEADS + 1, dtype=jnp.float32) / N_HEADS).reshape(N_KV, GQA_G)
    gather = jax.vmap(lambda a, i: a[i])

    def to_blocks(a):
        return jnp.swapaxes(a.reshape((B, nb, block) + a.shape[2:]), 0, 1)

    def one_block(args):
        qb, qib, wib, pb = args
        dots = jnp.einsum('bqhd,bsd->bqhs', qib, ki, preferred_element_type=jnp.float32) * IDX_DIM ** -0.5
        score = jnp.einsum('bqh,bqhs->bqs', wib.astype(jnp.float32), jax.nn.relu(dots))
        vis = k_chunk[None, :] <= (pb // CHUNK)[:, None]
        score = jnp.where(vis[None], score, -jnp.inf)
        top_val, top_idx = lax.top_k(score, topk)
        valid = jnp.isfinite(top_val)
        ks = gather(k, top_idx)
        vs = gather(v, top_idx)
        qg = qb.reshape(B, block, N_KV, GQA_G, HEAD_DIM)
        logits = jnp.einsum('bqngd,bqknd->bqngk', qg, ks, preferred_element_type=jnp.float32) * HEAD_DIM ** -0.5
        dist = jnp.abs(pb[None, :, None] - top_idx).astype(jnp.float32)
        logits = logits - slopes[None, None, :, :, None] * dist[:, :, None, None, :]
        logits = jnp.where(valid[:, :, None, None, :], logits, -jnp.inf)
        p = jax.nn.softmax(logits, axis=-1).astype(v.dtype)
        o = jnp.einsum('bqngk,bqknd->bqngd', p, vs)
        return o.reshape(B, block, N_HEADS * HEAD_DIM)

    out = lax.map(one_block, (to_blocks(q), to_blocks(qi), to_blocks(wi), q_pos.reshape(nb, block)))
    return jnp.swapaxes(out, 0, 1).reshape(B, T, N_HEADS * HEAD_DIM)


def trunk_layer(x, c, q_pos, cache, W):
    B, T, D = x.shape
    mod = jax.nn.silu(c) @ W['w_ada'] + W['b_ada']
    sh1, sc1, g1, sh2, sc2, g2 = jnp.split(mod[:, None, :], 6, axis=-1)
    h = rms_norm(x, W['g_norm1']) * (1 + sc1) + sh1
    a_in, b_in, q, k, v, qi, ki, wi, gl = split_cols(h @ W['w_in'])

    a = a_in[..., :D_CONV] * jax.nn.sigmoid(a_in[..., D_CONV:])
    hist = jnp.zeros((B, CONV_W - 1, D_CONV), a.dtype) if cache is None else cache[3]
    apad = jnp.concatenate([hist, a], axis=1)
    new_conv = apad[:, -(CONV_W - 1):]
    a = causal_dwconv(apad, W['w_dw'], W['b_dw'])
    y_a = jax.nn.silu(layer_norm(a, W['ln_c_g'], W['ln_c_b'])) @ W['w_pa']

    z = jax.nn.gelu(b_in)
    u = z[..., :D_GMLP]
    vg = layer_norm(z[..., D_GMLP:], W['ln_v_g'], W['ln_v_b'])
    L = GMLP_CHUNK if cache is None else T
    y_b = (u * spatial_gate(vg, W['w_s'], W['b_s'], L)) @ W['w_pb']

    q = rms_norm(q.reshape(B, T, N_HEADS, HEAD_DIM), W['g_q'])
    k = rms_norm(k.reshape(B, T, N_KV, HEAD_DIM), W['g_k'])
    v = v.reshape(B, T, N_KV, HEAD_DIM)
    qi = qi.reshape(B, T, IDX_HEADS, IDX_DIM)
    wi = wi * IDX_HEADS ** -0.5
    if cache is None:
        k_all, v_all, ki_all = k, v, ki
    else:
        k_all = jnp.concatenate([cache[0], k], axis=1)
        v_all = jnp.concatenate([cache[1], v], axis=1)
        ki_all = jnp.concatenate([cache[2], ki], axis=1)
    S = k_all.shape[1]
    topk = min(TOPK_MAX, S // 4)
    o = sparse_attention(q, k_all, v_all, qi, ki_all, wi, q_pos, topk, min(Q_BLOCK, T))
    y_c = o @ W['w_pc']

    gates = jax.nn.sigmoid(gl).reshape(B, T, N_BRANCH, D)
    m = gates[:, :, 0] * y_a + gates[:, :, 1] * y_b + gates[:, :, 2] * y_c
    x = x + g1 * (m @ W['w_o'])

    h2 = rms_norm(x, W['g_norm2']) * (1 + sc2) + sh2
    f = jnp.square(jax.nn.relu(h2 @ W['w_1'] + W['b_1'])) @ W['w_2'] + W['b_2']
    x = x + g2 * f
    return x, (k, v, ki, new_conv, vg)


def setup_inputs(seed: int = 0) -> dict:
    key = jax.random.key(seed)
    ks = iter(jax.random.split(key, 40))
    nrm = lambda shape, scale: scale * jax.random.normal(next(ks), shape, jnp.float32)
    D = D_MODEL
    return {
        'x_prompt': nrm((BATCH, SEQ, D), 1.0),
        'x_sample': nrm((DEC_BATCH, DEC_SEQ, D), 1.0),
        'c_prompt': nrm((BATCH, D), 1.0),
        'c_sample': nrm((DEC_BATCH, D), 1.0),
        'cache_k': nrm((DEPTH, DEC_BATCH, PAST_LEN, N_KV, HEAD_DIM), 1.0),
        'cache_v': nrm((DEPTH, DEC_BATCH, PAST_LEN, N_KV, HEAD_DIM), 1.0),
        'cache_kidx': nrm((DEPTH, DEC_BATCH, PAST_LEN, IDX_DIM), 1.0),
        'cache_conv': nrm((DEPTH, DEC_BATCH, CONV_W - 1, D_CONV), 0.5),
        'w_ada': nrm((DEPTH, D, 6 * D), 0.5 * D ** -0.5),
        'b_ada': nrm((DEPTH, 6 * D), 0.1),
        'g_norm1': 1.0 + nrm((DEPTH, D), 0.05),
        'w_in': nrm((DEPTH, D, IN_COLS), D ** -0.5),
        'w_dw': nrm((DEPTH, CONV_W, D_CONV), CONV_W ** -0.5),
        'b_dw': nrm((DEPTH, D_CONV), 0.02),
        'ln_c_g': 1.0 + nrm((DEPTH, D_CONV), 0.05),
        'ln_c_b': nrm((DEPTH, D_CONV), 0.02),
        'w_pa': nrm((DEPTH, D_CONV, D), D_CONV ** -0.5),
        'ln_v_g': 1.0 + nrm((DEPTH, D_GMLP), 0.05),
        'ln_v_b': nrm((DEPTH, D_GMLP), 0.02),
        'w_s': nrm((DEPTH, GMLP_GROUPS, GMLP_CHUNK, GMLP_CHUNK), GMLP_CHUNK ** -0.5),
        'b_s': 1.0 + nrm((DEPTH, GMLP_GROUPS, GMLP_CHUNK), 0.1),
        'w_pb': nrm((DEPTH, D_GMLP, D), D_GMLP ** -0.5),
        'g_q': 1.0 + nrm((DEPTH, HEAD_DIM), 0.05),
        'g_k': 1.0 + nrm((DEPTH, HEAD_DIM), 0.05),
        'w_pc': nrm((DEPTH, N_HEADS * HEAD_DIM, D), (N_HEADS * HEAD_DIM) ** -0.5),
        'w_o': nrm((DEPTH, D, D), D ** -0.5),
        'g_norm2': 1.0 + nrm((DEPTH, D), 0.05),
        'w_1': nrm((DEPTH, D, D_FF), D ** -0.5),
        'b_1': nrm((DEPTH, D_FF), 0.02),
        'w_2': nrm((DEPTH, D_FF, D), D_FF ** -0.5),
        'b_2': nrm((DEPTH, D), 0.02),
    }


def reference(x_prompt, x_sample, c_prompt, c_sample, cache_k, cache_v, cache_kidx, cache_conv,
              w_ada, b_ada, g_norm1, w_in, w_dw, b_dw, ln_c_g, ln_c_b, w_pa, ln_v_g, ln_v_b,
              w_s, b_s, w_pb, g_q, g_k, w_pc, w_o, g_norm2, w_1, b_1, w_2, b_2):
    pos_p = jnp.arange(x_prompt.shape[1], dtype=jnp.int32)
    pos_s = PAST_LEN + jnp.arange(x_sample.shape[1], dtype=jnp.int32)
    hp, hs = x_prompt, x_sample
    kp_l, vp_l, kip_l, cp_l = [], [], [], []
    ks_l, vs_l, kis_l, cs_l, gv_l = [], [], [], [], []
    for l in range(DEPTH):
        W = dict(w_ada=w_ada[l], b_ada=b_ada[l], g_norm1=g_norm1[l], w_in=w_in[l], w_dw=w_dw[l],
                 b_dw=b_dw[l], ln_c_g=ln_c_g[l], ln_c_b=ln_c_b[l], w_pa=w_pa[l], ln_v_g=ln_v_g[l],
                 ln_v_b=ln_v_b[l], w_s=w_s[l], b_s=b_s[l], w_pb=w_pb[l], g_q=g_q[l], g_k=g_k[l],
                 w_pc=w_pc[l], w_o=w_o[l], g_norm2=g_norm2[l], w_1=w_1[l], b_1=b_1[l], w_2=w_2[l],
                 b_2=b_2[l])
        hp, (kp, vp, kip, cp, _) = trunk_layer(hp, c_prompt, pos_p, None, W)
        hs, (ksn, vsn, kisn, csn, gvn) = trunk_layer(
            hs, c_sample, pos_s, (cache_k[l], cache_v[l], cache_kidx[l], cache_conv[l]), W)
        kp_l.append(kp); vp_l.append(vp); kip_l.append(kip); cp_l.append(cp)
        ks_l.append(ksn); vs_l.append(vsn); kis_l.append(kisn); cs_l.append(csn); gv_l.append(gvn)
    return (hp, hs,
            jnp.stack(kp_l), jnp.stack(vp_l), jnp.stack(kip_l), jnp.stack(cp_l),
            jnp.stack(ks_l), jnp.stack(vs_l), jnp.stack(kis_l), jnp.stack(cs_l), jnp.stack(gv_l))
```

```python
import functools

import jax
import jax.numpy as jnp
from jax import lax
from jax.experimental import pallas as pl
from jax.experimental.pallas import tpu as pltpu

D_MODEL = 1024
CHUNK = 64
CONV_W = 31
HIST = CONV_W - 1
GMLP_CHUNK = 128
GMLP_GROUPS = 8
GMLP_GW = D_MODEL // GMLP_GROUPS
N_HEADS = 8
HEAD_DIM = 128
N_KV = 2
GQA_G = N_HEADS // N_KV
IDX_HEADS = 8
IDX_DIM = 64
TOPK_MAX = 256
Q_BLOCK = 128
D_FF = 4 * D_MODEL
EPS = 1e-6

LANES = 128
SUBLANES = 8
HALO = 32
ROW_TILE = 512
VMEM_LIMIT = 56 * 1024 * 1024
INT_MIN = -2 ** 31
NEG = -1e30

f32 = jnp.float32
bf16 = jnp.bfloat16


def _cparams(n_axes):
    return pltpu.CompilerParams(dimension_semantics=("arbitrary",) * n_axes,
                                vmem_limit_bytes=VMEM_LIMIT)


def _tiles(B, T):
    if T >= ROW_TILE:
        assert T % ROW_TILE == 0
        return 1, ROW_TILE
    assert ROW_TILE % T == 0 and B % (ROW_TILE // T) == 0
    return ROW_TILE // T, T


def _row_spec(bb, tt, n):
    return pl.BlockSpec((bb, tt, n), lambda b, t: (b, t, 0))


def _batch_spec(bb, r, n):
    return pl.BlockSpec((bb, r, n), lambda b, t: (b, 0, 0))


def _const_spec(shape):
    nd = len(shape)
    return pl.BlockSpec(shape, lambda b, t: (0,) * nd, pipeline_mode=pl.Buffered(1))


def _mm(a, w):
    return jnp.dot(a, w, preferred_element_type=f32)


def _sigmoid(x):
    return 1.0 / (1.0 + jnp.exp(-x))


def _silu(x):
    return x * _sigmoid(x)


def _gelu_tanh(x):
    return 0.5 * x * (1.0 + jnp.tanh(0.7978845608028654 * (x + 0.044715 * (x * x * x))))


def _rms(x, g):
    return x * lax.rsqrt(jnp.mean(x * x, axis=-1, keepdims=True) + EPS) * g


def _ln(x, g, b):
    mu = jnp.mean(x, axis=-1, keepdims=True)
    xc = x - mu
    var = jnp.mean(xc * xc, axis=-1, keepdims=True)
    return xc * lax.rsqrt(var + EPS) * g + b


def _ada_kernel(c_ref, w_ref, b_ref, o_ref):
    c = c_ref[...]
    o_ref[0] = _mm(_silu(c).astype(bf16), w_ref[0]) + b_ref[0]


def _ada(c_all, w_ada, b_ada):
    depth, d, n = w_ada.shape
    bp = c_all.shape[0]
    tn = 1024
    return pl.pallas_call(
        _ada_kernel,
        grid=(depth, n // tn),
        in_specs=[pl.BlockSpec((bp, d), lambda l, j: (0, 0)),
                  pl.BlockSpec((1, d, tn), lambda l, j: (l, 0, j)),
                  pl.BlockSpec((1, 1, tn), lambda l, j: (l, 0, j))],
        out_specs=pl.BlockSpec((1, bp, tn), lambda l, j: (l, 0, j)),
        out_shape=jax.ShapeDtypeStruct((depth, bp, n), f32),
        compiler_params=_cparams(2),
    )(c_all, w_ada, b_ada.reshape(depth, 1, n))


def _prenorm_kernel(x_ref, g_ref, sc_ref, sh_ref, h_ref):
    h = _rms(x_ref[...], g_ref[...]) * (1.0 + sc_ref[...]) + sh_ref[...]
    h_ref[...] = h.astype(bf16)


def _prenorm(x, g, sc, sh):
    B, T, D = x.shape
    bb, tt = _tiles(B, T)
    return pl.pallas_call(
        _prenorm_kernel,
        grid=(B // bb, T // tt),
        in_specs=[_row_spec(bb, tt, D), _const_spec((1, 1, D)),
                  _batch_spec(bb, 1, D), _batch_spec(bb, 1, D)],
        out_specs=_row_spec(bb, tt, D),
        out_shape=jax.ShapeDtypeStruct((B, T, D), bf16),
        compiler_params=_cparams(2),
    )(x, g.reshape(1, 1, D), sc, sh)


CONV_ROWS = 64
CONV_COLS = 256


def _conv_kernel(h_ref, hist_ref, wa1_ref, wa2_ref, wdw_ref, bdw_ref, lng_ref, lnb_ref, wpa_ref,
                 ya_ref, newc_ref, apad_ref, conv_ref, *, bb, tt):
    D = D_MODEL
    t = pl.program_id(1)

    @pl.when(t == 0)
    def _():
        apad_ref[:, HALO - HIST:HALO, :] = hist_ref[...]

    h = h_ref[...].reshape(bb * tt, D)
    a = _mm(h, wa1_ref[...]) * _sigmoid(_mm(h, wa2_ref[...]))
    apad_ref[:, HALO:HALO + tt, :] = a.reshape(bb, tt, D)

    base = HALO - HIST
    for b in range(bb):
        for r0 in range(0, tt, CONV_ROWS):
            for c0 in range(0, D, CONV_COLS):
                acc = jnp.zeros((CONV_ROWS, CONV_COLS), f32)
                for j in range(CONV_W):
                    acc = acc + (wdw_ref[j:j + 1, c0:c0 + CONV_COLS]
                                 * apad_ref[b, base + r0 + j:base + r0 + j + CONV_ROWS, c0:c0 + CONV_COLS])
                conv_ref[b * tt + r0:b * tt + r0 + CONV_ROWS, c0:c0 + CONV_COLS] = acc

    tail = apad_ref[:, HALO + tt - HIST:HALO + tt, :]
    newc_ref[...] = tail
    apad_ref[:, HALO - HIST:HALO, :] = tail

    y = conv_ref[...] + bdw_ref[...]
    z = _silu(_ln(y, lng_ref[...], lnb_ref[...]))
    ya_ref[...] = _mm(z.astype(bf16), wpa_ref[...]).astype(bf16).reshape(bb, tt, D)


def _branch_conv(h, hist, wa1, wa2, wdw, bdw, lng, lnb, wpa):
    B, T, D = h.shape
    bb, tt = _tiles(B, T)
    kern = functools.partial(_conv_kernel, bb=bb, tt=tt)
    return pl.pallas_call(
        kern,
        grid=(B // bb, T // tt),
        in_specs=[_row_spec(bb, tt, D), _batch_spec(bb, HIST, D),
                  _const_spec((D, D)), _const_spec((D, D)), _const_spec((CONV_W, D)),
                  _const_spec((1, D)), _const_spec((1, D)), _const_spec((1, D)), _const_spec((D, D))],
        out_specs=[_row_spec(bb, tt, D), _batch_spec(bb, HIST, D)],
        out_shape=[jax.ShapeDtypeStruct((B, T, D), bf16), jax.ShapeDtypeStruct((B, HIST, D), f32)],
        scratch_shapes=[pltpu.VMEM((bb, HALO + tt, D), f32), pltpu.VMEM((bb * tt, D), f32)],
        compiler_params=_cparams(2),
    )(h, hist, wa1, wa2, wdw, bdw.reshape(1, D), lng.reshape(1, D), lnb.reshape(1, D), wpa)


def _gmlp_kernel(h_ref, wbu_ref, wbv_ref, lng_ref, lnb_ref, wm_ref, bs_ref, wpb_ref,
                 yb_ref, *rest, bb, tt, L, emit_v):
    D = D_MODEL
    if emit_v:
        vg_ref, s_ref = rest
    else:
        (s_ref,) = rest
    M = bb * tt
    h = h_ref[...].reshape(M, D)
    u = _gelu_tanh(_mm(h, wbu_ref[...]))
    vg = _ln(_gelu_tanh(_mm(h, wbv_ref[...])), lng_ref[...], lnb_ref[...])
    if emit_v:
        vg_ref[...] = vg.reshape(bb, tt, D)
    vgb = vg.astype(bf16)
    for c in range(M // L):
        for g in range(GMLP_GROUPS):
            blk = vgb[c * L:(c + 1) * L, g * GMLP_GW:(g + 1) * GMLP_GW]
            s_ref[c * L:(c + 1) * L, g * GMLP_GW:(g + 1) * GMLP_GW] = _mm(wm_ref[g], blk) + bs_ref[:, g * GMLP_GW:(g + 1) * GMLP_GW]
    yb_ref[...] = _mm((u * s_ref[...]).astype(bf16), wpb_ref[...]).astype(bf16).reshape(bb, tt, D)


def _branch_gmlp(h, wbu, wbv, lng, lnb, w_s, b_s, wpb, L, emit_v):
    B, T, D = h.shape
    bb, tt = _tiles(B, T)
    assert tt % L == 0
    tril = jnp.tril(jnp.ones((L, L), dtype=bool))
    wm = jnp.where(tril[None], w_s[:, :L, :L], 0).astype(bf16)
    bs = jnp.repeat(jnp.swapaxes(b_s[:, :L], 0, 1), GMLP_GW, axis=1)
    kern = functools.partial(_gmlp_kernel, bb=bb, tt=tt, L=L, emit_v=emit_v)
    out_specs = [_row_spec(bb, tt, D)]
    out_shape = [jax.ShapeDtypeStruct((B, T, D), bf16)]
    if emit_v:
        out_specs.append(_row_spec(bb, tt, D))
        out_shape.append(jax.ShapeDtypeStruct((B, T, D), f32))
    return pl.pallas_call(
        kern,
        grid=(B // bb, T // tt),
        in_specs=[_row_spec(bb, tt, D), _const_spec((D, D)), _const_spec((D, D)),
                  _const_spec((1, D)), _const_spec((1, D)), _const_spec((GMLP_GROUPS, L, L)),
                  _const_spec((L, D)), _const_spec((D, D))],
        out_specs=out_specs,
        out_shape=out_shape,
        scratch_shapes=[pltpu.VMEM((bb * tt, D), f32)],
        compiler_params=_cparams(2),
    )(h, wbu, wbv, lng.reshape(1, D), lnb.reshape(1, D), wm, bs, wpb)


QKV_COLS = N_HEADS * HEAD_DIM + 2 * N_KV * HEAD_DIM + IDX_HEADS * IDX_DIM + LANES


def _attn_proj_kernel(h_ref, w_ref, gq_ref, gk_ref, q_ref, k_ref, v_ref, qi_ref, ki_ref, wi_ref, *, bb, tt):
    D = D_MODEL
    M = bb * tt
    p = _mm(h_ref[...].reshape(M, D), w_ref[...])
    o = 0
    for hd in range(N_HEADS):
        q_ref[:, :, hd * HEAD_DIM:(hd + 1) * HEAD_DIM] = _rms(
            p[:, o:o + HEAD_DIM], gq_ref[...]).astype(bf16).reshape(bb, tt, HEAD_DIM)
        o += HEAD_DIM
    for n in range(N_KV):
        k_ref[:, :, n * HEAD_DIM:(n + 1) * HEAD_DIM] = _rms(
            p[:, o:o + HEAD_DIM], gk_ref[...]).reshape(bb, tt, HEAD_DIM)
        o += HEAD_DIM
    v_ref[...] = p[:, o:o + N_KV * HEAD_DIM].reshape(bb, tt, N_KV * HEAD_DIM)
    o += N_KV * HEAD_DIM
    qi_ref[...] = p[:, o:o + IDX_HEADS * IDX_DIM].astype(bf16).reshape(bb, tt, IDX_HEADS * IDX_DIM)
    o += IDX_HEADS * IDX_DIM
    ki_ref[...] = p[:, o:o + IDX_DIM].reshape(bb, tt, IDX_DIM)
    o += IDX_DIM
    wi_ref[...] = (p[:, o:o + IDX_HEADS] * IDX_HEADS ** -0.5).reshape(bb, tt, IDX_HEADS)


def _attn_proj(h, w_c, g_q, g_k):
    B, T, D = h.shape
    bb, tt = _tiles(B, T)
    kern = functools.partial(_attn_proj_kernel, bb=bb, tt=tt)
    widths = (N_HEADS * HEAD_DIM, N_KV * HEAD_DIM, N_KV * HEAD_DIM, IDX_HEADS * IDX_DIM, IDX_DIM, IDX_HEADS)
    dtypes = (bf16, f32, f32, bf16, f32, f32)
    return pl.pallas_call(
        kern,
        grid=(B // bb, T // tt),
        in_specs=[_row_spec(bb, tt, D), _const_spec((D, QKV_COLS)),
                  _const_spec((1, HEAD_DIM)), _const_spec((1, HEAD_DIM))],
        out_specs=[_row_spec(bb, tt, n) for n in widths],
        out_shape=[jax.ShapeDtypeStruct((B, T, n), dt) for n, dt in zip(widths, dtypes)],
        compiler_params=_cparams(2),
    )(h, w_c, g_q.reshape(1, HEAD_DIM), g_k.reshape(1, HEAD_DIM))


def _sparse_attn_kernel(q_ref, qi_ref, wi_ref, k_ref, v_ref, ki_ref, o_ref,
                        key_ref, m_ref, l_ref, acc_ref, *, tq, ck, S, pos0, topk):
    qb = pl.program_id(1)
    p_first = pos0 + qb * tq
    limit = jnp.minimum(S, ((p_first + tq - 1) // CHUNK + 1) * CHUNK)
    nkc = (limit + ck - 1) // ck
    qpos = p_first + lax.broadcasted_iota(jnp.int32, (tq, 1), 0)
    qchunk = qpos // CHUNK
    lane_pos = lax.broadcasted_iota(jnp.int32, (tq, ck), 1)
    wi = wi_ref[0]
    qi = qi_ref[0]

    def score_body(c, carry):
        start = pl.multiple_of(c * ck, ck)
        kic = ki_ref[0, pl.ds(start, ck), :]
        sc = jnp.zeros((tq, ck), f32)
        for hd in range(IDX_HEADS):
            d = lax.dot_general(qi[:, hd * IDX_DIM:(hd + 1) * IDX_DIM], kic,
                                (((1,), (1,)), ((), ())), preferred_element_type=f32)
            sc = sc + wi[:, hd:hd + 1] * jnp.maximum(d * IDX_DIM ** -0.5, 0.0)
        kpos = start + lane_pos
        vis = (kpos // CHUNK <= qchunk) & (kpos < S)
        bits = pltpu.bitcast(sc, jnp.int32)
        key = bits ^ ((bits >> 31) & 0x7FFFFFFF)
        key_ref[c] = jnp.where(vis, key, INT_MIN)
        return carry

    lax.fori_loop(0, nkc, score_body, 0)

    def count(pred):
        def body(c, part):
            hit = pred(key_ref[c], c)
            for s in range(ck // LANES):
                part = part + hit[:, s * LANES:(s + 1) * LANES].astype(jnp.int32)
            return part
        part = lax.fori_loop(0, nkc, body, jnp.zeros((tq, LANES), jnp.int32))
        return jnp.sum(part, axis=1, keepdims=True)

    def bit_body(i, prefix):
        cand = prefix ^ lax.shift_left(jnp.int32(1), 31 - i)
        cnt = count(lambda kc, c: kc >= cand)
        return jnp.where(cnt >= topk, cand, prefix)

    thr = lax.fori_loop(0, 32, bit_body, jnp.full((tq, 1), INT_MIN, jnp.int32))

    real = thr != INT_MIN
    need = topk - count(lambda kc, c: kc > thr)
    n_eq = count(lambda kc, c: kc == thr)
    excess = real & (n_eq > need)
    pos_bits = max(1, (ck * key_ref.shape[0] - 1).bit_length())

    def tie_search():
        def pbody(i, ans):
            cand = ans | lax.shift_left(jnp.int32(1), pos_bits - 1 - i)
            cnt = count(lambda kc, c: (kc == thr) & (c * ck + lane_pos < cand))
            return jnp.where(cnt < need, cand, ans)
        return lax.fori_loop(0, pos_bits, pbody, jnp.zeros((tq, 1), jnp.int32))

    any_excess = jnp.max(excess.astype(jnp.int32)) > 0
    tie_lim = lax.cond(any_excess, tie_search, lambda: jnp.zeros((tq, 1), jnp.int32))
    big = jnp.int32(ck * key_ref.shape[0])
    tie_lim = jnp.where(excess, tie_lim, jnp.where(real, big, -1))

    m_ref[...] = jnp.full(m_ref.shape, NEG, f32)
    l_ref[...] = jnp.zeros(l_ref.shape, f32)
    acc_ref[...] = jnp.zeros(acc_ref.shape, f32)
    scale = HEAD_DIM ** -0.5

    def attn_body(c, carry):
        start = pl.multiple_of(c * ck, ck)
        kc = key_ref[c]
        kpos = start + lane_pos
        sel = (kc > thr) | ((kc == thr) & (kpos <= tie_lim))
        dist = jnp.abs(qpos - kpos).astype(f32)
        for n in range(N_KV):
            kn = k_ref[0, pl.ds(start, ck), n * HEAD_DIM:(n + 1) * HEAD_DIM]
            vn = v_ref[0, pl.ds(start, ck), n * HEAD_DIM:(n + 1) * HEAD_DIM]
            for g in range(GQA_G):
                hd = n * GQA_G + g
                slope = 2.0 ** (-8.0 * (hd + 1) / N_HEADS)
                qh = q_ref[0, :, hd * HEAD_DIM:(hd + 1) * HEAD_DIM]
                s = lax.dot_general(qh, kn, (((1,), (1,)), ((), ())), preferred_element_type=f32)
                s = jnp.where(sel, s * scale - slope * dist, NEG)
                m_prev = m_ref[hd]
                m_new = jnp.maximum(m_prev, jnp.max(s, axis=1, keepdims=True))
                alpha = jnp.exp(m_prev - m_new)
                p = jnp.exp(s - m_new)
                l_ref[hd] = alpha * l_ref[hd] + jnp.sum(p, axis=1, keepdims=True)
                acc_ref[hd] = alpha * acc_ref[hd] + _mm(p.astype(bf16), vn)
                m_ref[hd] = m_new
        return carry

    lax.fori_loop(0, nkc, attn_body, 0)
    for hd in range(N_HEADS):
        o_ref[0, :, hd * HEAD_DIM:(hd + 1) * HEAD_DIM] = (acc_ref[hd] / l_ref[hd]).astype(bf16)


def _sparse_attn(q, qi, wi, k_all, v_all, ki_all, S, pos0, tq, ck, topk):
    B, T, _ = q.shape
    Sp = k_all.shape[1]
    assert Sp % ck == 0 and T % tq == 0
    kern = functools.partial(_sparse_attn_kernel, tq=tq, ck=ck, S=S, pos0=pos0, topk=topk)
    hq = N_HEADS * HEAD_DIM
    return pl.pallas_call(
        kern,
        grid=(B, T // tq),
        in_specs=[pl.BlockSpec((1, tq, hq), lambda b, t: (b, t, 0)),
                  pl.BlockSpec((1, tq, IDX_HEADS * IDX_DIM), lambda b, t: (b, t, 0)),
                  pl.BlockSpec((1, tq, IDX_HEADS), lambda b, t: (b, t, 0)),
                  pl.BlockSpec((1, Sp, N_KV * HEAD_DIM), lambda b, t: (b, 0, 0)),
                  pl.BlockSpec((1, Sp, N_KV * HEAD_DIM), lambda b, t: (b, 0, 0)),
                  pl.BlockSpec((1, Sp, IDX_DIM), lambda b, t: (b, 0, 0))],
        out_specs=pl.BlockSpec((1, tq, hq), lambda b, t: (b, t, 0)),
        out_shape=jax.ShapeDtypeStruct((B, T, hq), bf16),
        scratch_shapes=[pltpu.VMEM((Sp // ck, tq, ck), jnp.int32),
                        pltpu.VMEM((N_HEADS, tq, 1), f32),
                        pltpu.VMEM((N_HEADS, tq, 1), f32),
                        pltpu.VMEM((N_HEADS, tq, HEAD_DIM), f32)],
        compiler_params=_cparams(2),
    )(q, qi, wi, k_all, v_all, ki_all)


def _merge_kernel(x_ref, h_ref, ya_ref, yb_ref, o_ref, g1_ref, wgl_ref, wpc_ref, wo_ref,
                  gn2_ref, sc2_ref, sh2_ref, x1_ref, h2_ref, *, bb, tt):
    D = D_MODEL
    M = bb * tt
    h = h_ref[...].reshape(M, D)
    yc = _mm(o_ref[...].reshape(M, D), wpc_ref[...])
    m = _sigmoid(_mm(h, wgl_ref[:, 0:D])) * ya_ref[...].reshape(M, D).astype(f32)
    m = m + _sigmoid(_mm(h, wgl_ref[:, D:2 * D])) * yb_ref[...].reshape(M, D).astype(f32)
    m = m + _sigmoid(_mm(h, wgl_ref[:, 2 * D:3 * D])) * yc
    x1 = x_ref[...] + g1_ref[...] * _mm(m.astype(bf16), wo_ref[...]).reshape(bb, tt, D)
    x1_ref[...] = x1
    h2 = _rms(x1, gn2_ref[...]) * (1.0 + sc2_ref[...]) + sh2_ref[...]
    h2_ref[...] = h2.astype(bf16)


def _merge(x, h, ya, yb, o, g1, wgl, wpc, wo, gn2, sc2, sh2):
    B, T, D = x.shape
    bb, tt = _tiles(B, T)
    kern = functools.partial(_merge_kernel, bb=bb, tt=tt)
    row = _row_spec(bb, tt, D)
    vec = _batch_spec(bb, 1, D)
    return pl.pallas_call(
        kern,
        grid=(B // bb, T // tt),
        in_specs=[row, row, row, row, row, vec, _const_spec((D, 3 * D)), _const_spec((D, D)),
                  _const_spec((D, D)), _const_spec((1, 1, D)), vec, vec],
        out_specs=[row, row],
        out_shape=[jax.ShapeDtypeStruct((B, T, D), f32), jax.ShapeDtypeStruct((B, T, D), bf16)],
        compiler_params=_cparams(2),
    )(x, h, ya, yb, o, g1, wgl, wpc, wo, gn2.reshape(1, 1, D), sc2, sh2)


FF_TILE = 1024


def _mlp_kernel(x1_ref, h2_ref, g2_ref, w1_ref, b1_ref, w2_ref, b2_ref, x2_ref, *, bb, tt):
    D = D_MODEL
    M = bb * tt
    h2 = h2_ref[...].reshape(M, D)
    acc = jnp.zeros((M, D), f32)
    for c0 in range(0, D_FF, FF_TILE):
        f = jnp.maximum(_mm(h2, w1_ref[:, c0:c0 + FF_TILE]) + b1_ref[:, c0:c0 + FF_TILE], 0.0)
        acc = acc + _mm((f * f).astype(bf16), w2_ref[c0:c0 + FF_TILE, :])
    f = (acc + b2_ref[...]).reshape(bb, tt, D)
    x2_ref[...] = x1_ref[...] + g2_ref[...] * f


def _mlp(x1, h2, g2, w1, b1, w2, b2):
    B, T, D = x1.shape
    bb, tt = _tiles(B, T)
    kern = functools.partial(_mlp_kernel, bb=bb, tt=tt)
    row = _row_spec(bb, tt, D)
    return pl.pallas_call(
        kern,
        grid=(B // bb, T // tt),
        in_specs=[row, row, _batch_spec(bb, 1, D), _const_spec((D, D_FF)), _const_spec((1, D_FF)),
                  _const_spec((D_FF, D)), _const_spec((1, D))],
        out_specs=row,
        out_shape=jax.ShapeDtypeStruct((B, T, D), f32),
        compiler_params=_cparams(2),
    )(x1, h2, g2, w1, b1.reshape(1, D_FF), w2, b2.reshape(1, D))


def _split_w_in(w_in):
    D = D_MODEL
    w = w_in.astype(bf16)
    o_b = 2 * D
    o_q = 4 * D
    o_gl = o_q + N_HEADS * HEAD_DIM + 2 * N_KV * HEAD_DIM + IDX_HEADS * IDX_DIM + IDX_DIM + IDX_HEADS
    n_c = o_gl - o_q
    w_c = jnp.pad(w[:, o_q:o_gl], ((0, 0), (0, QKV_COLS - n_c)))
    return dict(wa1=w[:, 0:D], wa2=w[:, D:2 * D], wbu=w[:, o_b:o_b + D], wbv=w[:, o_b + D:o_b + 2 * D],
                w_c=w_c, wgl=w[:, o_gl:o_gl + 3 * D])


def _layer(x, mod, cache, W, pos0):
    B, T, D = x.shape
    sh1, sc1, g1, sh2, sc2, g2 = mod
    win = W['w_in']
    h = _prenorm(x, W['g_norm1'], sc1, sh1)

    hist = jnp.zeros((B, HIST, D), f32) if cache is None else cache[3]
    ya, new_conv = _branch_conv(h, hist, win['wa1'], win['wa2'], W['w_dw'], W['b_dw'],
                                W['ln_c_g'], W['ln_c_b'], W['w_pa'])

    L = GMLP_CHUNK if cache is None else T
    res = _branch_gmlp(h, win['wbu'], win['wbv'], W['ln_v_g'], W['ln_v_b'], W['w_s'], W['b_s'],
                       W['w_pb'], L, cache is not None)
    yb = res[0]
    vg = res[1] if cache is not None else None

    q, k, v, qi, ki, wi = _attn_proj(h, win['w_c'], W['g_q'], W['g_k'])
    if cache is None:
        k_all, v_all, ki_all = k, v, ki
        S = T
        ck = 512
    else:
        k_all = jnp.concatenate([cache[0].reshape(B, -1, N_KV * HEAD_DIM), k], axis=1)
        v_all = jnp.concatenate([cache[1].reshape(B, -1, N_KV * HEAD_DIM), v], axis=1)
        ki_all = jnp.concatenate([cache[2], ki], axis=1)
        S = k_all.shape[1]
        ck = 384
    Sp = -(-S // ck) * ck
    padk = lambda a: jnp.pad(a.astype(bf16), ((0, 0), (0, Sp - S), (0, 0)))
    topk = min(TOPK_MAX, S // 4)
    o = _sparse_attn(q, qi, wi, padk(k_all), padk(v_all), padk(ki_all), S, pos0, min(Q_BLOCK, T), ck, topk)

    x1, h2 = _merge(x, h, ya, yb, o, g1, win['wgl'], W['w_pc'], W['w_o'], W['g_norm2'], sc2, sh2)
    x2 = _mlp(x1, h2, g2, W['w_1'], W['b_1'], W['w_2'], W['b_2'])
    return x2, (k.reshape(B, T, N_KV, HEAD_DIM), v.reshape(B, T, N_KV, HEAD_DIM), ki, new_conv, vg)


def kernel(x_prompt, x_sample, c_prompt, c_sample, cache_k, cache_v, cache_kidx, cache_conv, w_ada, b_ada, g_norm1, w_in, w_dw, b_dw, ln_c_g, ln_c_b, w_pa, ln_v_g, ln_v_b, w_s, b_s, w_pb, g_q, g_k, w_pc, w_o, g_norm2, w_1, b_1, w_2, b_2):
    depth = w_in.shape[0]
    D = D_MODEL
    Bp, Bs = x_prompt.shape[0], x_sample.shape[0]
    past = cache_k.shape[2]
    c_all = jnp.concatenate([c_prompt, c_sample], axis=0)
    n_c = c_all.shape[0]
    n_pad = -(-n_c // SUBLANES) * SUBLANES
    mod_all = _ada(jnp.pad(c_all, ((0, n_pad - n_c), (0, 0))), w_ada.astype(bf16), b_ada)

    def mods(l, lo, n):
        m = mod_all[l, lo:lo + n]
        return [m[:, None, i * D:(i + 1) * D] for i in range(6)]

    hp, hs = x_prompt, x_sample
    outs_p, outs_s = [], []
    for l in range(depth):
        W = dict(g_norm1=g_norm1[l], w_in=_split_w_in(w_in[l]), w_dw=w_dw[l], b_dw=b_dw[l],
                 ln_c_g=ln_c_g[l], ln_c_b=ln_c_b[l], w_pa=w_pa[l].astype(bf16), ln_v_g=ln_v_g[l],
                 ln_v_b=ln_v_b[l], w_s=w_s[l], b_s=b_s[l], w_pb=w_pb[l].astype(bf16), g_q=g_q[l],
                 g_k=g_k[l], w_pc=w_pc[l].astype(bf16), w_o=w_o[l].astype(bf16), g_norm2=g_norm2[l],
                 w_1=w_1[l].astype(bf16), b_1=b_1[l], w_2=w_2[l].astype(bf16), b_2=b_2[l])
        hp, op = _layer(hp, mods(l, 0, Bp), None, W, 0)
        hs, os_ = _layer(hs, mods(l, Bp, Bs), (cache_k[l], cache_v[l], cache_kidx[l], cache_conv[l]), W, past)
        outs_p.append(op)
        outs_s.append(os_)
    stack = lambda outs, i: jnp.stack([o[i] for o in outs])
    return (hp, hs,
            stack(outs_p, 0), stack(outs_p, 1), stack(outs_p, 2), stack(outs_p, 3),
            stack(outs_s, 0), stack(outs_s, 1), stack(outs_s, 2), stack(outs_s, 3), stack(outs_s, 4))
```

```python
import functools

import jax
import jax.numpy as jnp
from jax import lax
from jax.experimental import pallas as pl
from jax.experimental.pallas import tpu as pltpu

D_MODEL = 1024
CHUNK = 64
CONV_W = 31
HIST = CONV_W - 1
GMLP_CHUNK = 128
GMLP_GROUPS = 8
GMLP_GW = D_MODEL // GMLP_GROUPS
N_HEADS = 8
HEAD_DIM = 128
N_KV = 2
GQA_G = N_HEADS // N_KV
IDX_HEADS = 8
IDX_DIM = 64
TOPK_MAX = 256
Q_BLOCK = 128
D_FF = 4 * D_MODEL
EPS = 1e-6

LANES = 128
SUBLANES = 8
HALO = 32
ROW_TILE = 512
VMEM_LIMIT = 56 * 1024 * 1024
INT_MIN = -2 ** 31
NEG = -1e30

f32 = jnp.float32
bf16 = jnp.bfloat16


def _cparams(n_axes):
    return pltpu.CompilerParams(dimension_semantics=("arbitrary",) * n_axes,
                                vmem_limit_bytes=VMEM_LIMIT)


def _tiles(B, T):
    if T >= ROW_TILE:
        assert T % ROW_TILE == 0
        return 1, ROW_TILE
    assert ROW_TILE % T == 0 and B % (ROW_TILE // T) == 0
    return ROW_TILE // T, T


def _row_spec(bb, tt, n):
    return pl.BlockSpec((bb, tt, n), lambda b, t: (b, t, 0))


def _batch_spec(bb, r, n):
    return pl.BlockSpec((bb, r, n), lambda b, t: (b, 0, 0))


def _const_spec(shape):
    nd = len(shape)
    return pl.BlockSpec(shape, lambda b, t: (0,) * nd, pipeline_mode=pl.Buffered(1))


def _mm(a, w):
    return jnp.dot(a, w, preferred_element_type=f32)


def _sigmoid(x):
    return 1.0 / (1.0 + jnp.exp(-x))


def _silu(x):
    return x * _sigmoid(x)


def _gelu_tanh(x):
    return 0.5 * x * (1.0 + jnp.tanh(0.7978845608028654 * (x + 0.044715 * (x * x * x))))


def _rms(x, g):
    return x * lax.rsqrt(jnp.mean(x * x, axis=-1, keepdims=True) + EPS) * g


def _ln(x, g, b):
    mu = jnp.mean(x, axis=-1, keepdims=True)
    xc = x - mu
    var = jnp.mean(xc * xc, axis=-1, keepdims=True)
    return xc * lax.rsqrt(var + EPS) * g + b


def _ada_kernel(c_ref, w_ref, b_ref, o_ref):
    c = c_ref[...]
    o_ref[0] = _mm(_silu(c).astype(bf16), w_ref[0]) + b_ref[0]


def _ada(c_all, w_ada, b_ada):
    depth, d, n = w_ada.shape
    bp = c_all.shape[0]
    tn = 1024
    return pl.pallas_call(
        _ada_kernel,
        grid=(depth, n // tn),
        in_specs=[pl.BlockSpec((bp, d), lambda l, j: (0, 0)),
                  pl.BlockSpec((1, d, tn), lambda l, j: (l, 0, j)),
                  pl.BlockSpec((1, 1, tn), lambda l, j: (l, 0, j))],
        out_specs=pl.BlockSpec((1, bp, tn), lambda l, j: (l, 0, j)),
        out_shape=jax.ShapeDtypeStruct((depth, bp, n), f32),
        compiler_params=_cparams(2),
    )(c_all, w_ada, b_ada.reshape(depth, 1, n))


def _prenorm_kernel(x_ref, g_ref, sc_ref, sh_ref, h_ref):
    h = _rms(x_ref[...], g_ref[...]) * (1.0 + sc_ref[...]) + sh_ref[...]
    h_ref[...] = h.astype(bf16)


def _prenorm(x, g, sc, sh):
    B, T, D = x.shape
    bb, tt = _tiles(B, T)
    return pl.pallas_call(
        _prenorm_kernel,
        grid=(B // bb, T // tt),
        in_specs=[_row_spec(bb, tt, D), _const_spec((1, 1, D)),
                  _batch_spec(bb, 1, D), _batch_spec(bb, 1, D)],
        out_specs=_row_spec(bb, tt, D),
        out_shape=jax.ShapeDtypeStruct((B, T, D), bf16),
        compiler_params=_cparams(2),
    )(x, g.reshape(1, 1, D), sc, sh)


CONV_ROWS = 64
CONV_COLS = 256


def _conv_kernel(h_ref, hist_ref, wa1_ref, wa2_ref, wdw_ref, bdw_ref, lng_ref, lnb_ref, wpa_ref,
                 ya_ref, newc_ref, apad_ref, conv_ref, shift_ref, *, bb, tt):
    D = D_MODEL
    t = pl.program_id(1)

    @pl.when(t == 0)
    def _():
        apad_ref[:, 0:HALO - HIST, :] = jnp.zeros((bb, HALO - HIST, D), f32)
        apad_ref[:, HALO - HIST:HALO, :] = hist_ref[...]

    h = h_ref[...].reshape(bb * tt, D)
    a = _mm(h, wa1_ref[...]) * _sigmoid(_mm(h, wa2_ref[...]))
    apad_ref[:, HALO:HALO + tt, :] = a.reshape(bb, tt, D)

    base = HALO - HIST
    n_sh = HALO + tt - SUBLANES
    for b in range(bb):
        for c0 in range(0, D, CONV_COLS):
            cols = slice(c0, c0 + CONV_COLS)
            for s in range(1, SUBLANES):
                shift_ref[s - 1, 0:n_sh, :] = apad_ref[b, s:s + n_sh, cols]
            for r0 in range(0, tt, CONV_ROWS):
                acc = jnp.zeros((CONV_ROWS, CONV_COLS), f32)
                for j in range(CONV_W):
                    q8, s = divmod(base + j, SUBLANES)
                    lo = r0 + SUBLANES * q8
                    src = (apad_ref[b, lo:lo + CONV_ROWS, cols] if s == 0
                           else shift_ref[s - 1, lo:lo + CONV_ROWS, :])
                    acc = acc + wdw_ref[j:j + 1, cols] * src
                conv_ref[b * tt + r0:b * tt + r0 + CONV_ROWS, cols] = acc

    tail = apad_ref[:, HALO + tt - HIST:HALO + tt, :]
    newc_ref[...] = tail
    apad_ref[:, HALO - HIST:HALO, :] = tail

    y = conv_ref[...] + bdw_ref[...]
    z = _silu(_ln(y, lng_ref[...], lnb_ref[...]))
    ya_ref[...] = _mm(z.astype(bf16), wpa_ref[...]).astype(bf16).reshape(bb, tt, D)


def _branch_conv(h, hist, wa1, wa2, wdw, bdw, lng, lnb, wpa):
    B, T, D = h.shape
    bb, tt = _tiles(B, T)
    kern = functools.partial(_conv_kernel, bb=bb, tt=tt)
    return pl.pallas_call(
        kern,
        grid=(B // bb, T // tt),
        in_specs=[_row_spec(bb, tt, D), _batch_spec(bb, HIST, D),
                  _const_spec((D, D)), _const_spec((D, D)), _const_spec((CONV_W, D)),
                  _const_spec((1, D)), _const_spec((1, D)), _const_spec((1, D)), _const_spec((D, D))],
        out_specs=[_row_spec(bb, tt, D), _batch_spec(bb, HIST, D)],
        out_shape=[jax.ShapeDtypeStruct((B, T, D), bf16), jax.ShapeDtypeStruct((B, HIST, D), f32)],
        scratch_shapes=[pltpu.VMEM((bb, HALO + tt, D), f32), pltpu.VMEM((bb * tt, D), f32),
                        pltpu.VMEM((SUBLANES - 1, HALO + tt, CONV_COLS), f32)],
        compiler_params=_cparams(2),
    )(h, hist, wa1, wa2, wdw, bdw.reshape(1, D), lng.reshape(1, D), lnb.reshape(1, D), wpa)


def _gmlp_kernel(h_ref, wbu_ref, wbv_ref, lng_ref, lnb_ref, wm_ref, bs_ref, wpb_ref,
                 yb_ref, *rest, bb, tt, L, emit_v):
    D = D_MODEL
    if emit_v:
        vg_ref, s_ref = rest
    else:
        (s_ref,) = rest
    M = bb * tt
    h = h_ref[...].reshape(M, D)
    u = _gelu_tanh(_mm(h, wbu_ref[...]))
    vg = _ln(_gelu_tanh(_mm(h, wbv_ref[...])), lng_ref[...], lnb_ref[...])
    if emit_v:
        vg_ref[...] = vg.reshape(bb, tt, D)
    vgb = vg.astype(bf16)
    for c in range(M // L):
        for g in range(GMLP_GROUPS):
            blk = vgb[c * L:(c + 1) * L, g * GMLP_GW:(g + 1) * GMLP_GW]
            s_ref[c * L:(c + 1) * L, g * GMLP_GW:(g + 1) * GMLP_GW] = _mm(wm_ref[g], blk) + bs_ref[:, g * GMLP_GW:(g + 1) * GMLP_GW]
    yb_ref[...] = _mm((u * s_ref[...]).astype(bf16), wpb_ref[...]).astype(bf16).reshape(bb, tt, D)


def _branch_gmlp(h, wbu, wbv, lng, lnb, w_s, b_s, wpb, L, emit_v):
    B, T, D = h.shape
    bb, tt = _tiles(B, T)
    assert tt % L == 0
    tril = jnp.tril(jnp.ones((L, L), dtype=bool))
    wm = jnp.where(tril[None], w_s[:, :L, :L], 0).astype(bf16)
    bs = jnp.repeat(jnp.swapaxes(b_s[:, :L], 0, 1), GMLP_GW, axis=1)
    kern = functools.partial(_gmlp_kernel, bb=bb, tt=tt, L=L, emit_v=emit_v)
    out_specs = [_row_spec(bb, tt, D)]
    out_shape = [jax.ShapeDtypeStruct((B, T, D), bf16)]
    if emit_v:
        out_specs.append(_row_spec(bb, tt, D))
        out_shape.append(jax.ShapeDtypeStruct((B, T, D), f32))
    return pl.pallas_call(
        kern,
        grid=(B // bb, T // tt),
        in_specs=[_row_spec(bb, tt, D), _const_spec((D, D)), _const_spec((D, D)),
                  _const_spec((1, D)), _const_spec((1, D)), _const_spec((GMLP_GROUPS, L, L)),
                  _const_spec((L, D)), _const_spec((D, D))],
        out_specs=out_specs,
        out_shape=out_shape,
        scratch_shapes=[pltpu.VMEM((bb * tt, D), f32)],
        compiler_params=_cparams(2),
    )(h, wbu, wbv, lng.reshape(1, D), lnb.reshape(1, D), wm, bs, wpb)


QKV_COLS = N_HEADS * HEAD_DIM + 2 * N_KV * HEAD_DIM + IDX_HEADS * IDX_DIM + LANES


def _attn_proj_kernel(h_ref, w_ref, gq_ref, gk_ref, q_ref, k_ref, v_ref, qi_ref, ki_ref, wi_ref, *, bb, tt):
    D = D_MODEL
    M = bb * tt
    p = _mm(h_ref[...].reshape(M, D), w_ref[...])
    o = 0
    for hd in range(N_HEADS):
        q_ref[:, :, hd * HEAD_DIM:(hd + 1) * HEAD_DIM] = (_rms(
            p[:, o:o + HEAD_DIM], gq_ref[...]) * HEAD_DIM ** -0.5).astype(bf16).reshape(bb, tt, HEAD_DIM)
        o += HEAD_DIM
    for n in range(N_KV):
        k_ref[:, :, n * HEAD_DIM:(n + 1) * HEAD_DIM] = _rms(
            p[:, o:o + HEAD_DIM], gk_ref[...]).reshape(bb, tt, HEAD_DIM)
        o += HEAD_DIM
    v_ref[...] = p[:, o:o + N_KV * HEAD_DIM].reshape(bb, tt, N_KV * HEAD_DIM)
    o += N_KV * HEAD_DIM
    qi_ref[...] = p[:, o:o + IDX_HEADS * IDX_DIM].astype(bf16).reshape(bb, tt, IDX_HEADS * IDX_DIM)
    o += IDX_HEADS * IDX_DIM
    ki_ref[...] = p[:, o:o + IDX_DIM].reshape(bb, tt, IDX_DIM)
    o += IDX_DIM
    wi_ref[...] = (p[:, o:o + IDX_HEADS] * IDX_HEADS ** -0.5 * IDX_DIM ** -0.5).reshape(bb, tt, IDX_HEADS)


def _attn_proj(h, w_c, g_q, g_k):
    B, T, D = h.shape
    bb, tt = _tiles(B, T)
    kern = functools.partial(_attn_proj_kernel, bb=bb, tt=tt)
    widths = (N_HEADS * HEAD_DIM, N_KV * HEAD_DIM, N_KV * HEAD_DIM, IDX_HEADS * IDX_DIM, IDX_DIM, IDX_HEADS)
    dtypes = (bf16, f32, f32, bf16, f32, f32)
    return pl.pallas_call(
        kern,
        grid=(B // bb, T // tt),
        in_specs=[_row_spec(bb, tt, D), _const_spec((D, QKV_COLS)),
                  _const_spec((1, HEAD_DIM)), _const_spec((1, HEAD_DIM))],
        out_specs=[_row_spec(bb, tt, n) for n in widths],
        out_shape=[jax.ShapeDtypeStruct((B, T, n), dt) for n, dt in zip(widths, dtypes)],
        compiler_params=_cparams(2),
    )(h, w_c, g_q.reshape(1, HEAD_DIM), g_k.reshape(1, HEAD_DIM))


PLANE_BITS = 32
BUTTERFLY = ((16, 0x0000FFFF), (8, 0x00FF00FF), (4, 0x0F0F0F0F), (2, 0x33333333), (1, 0x55555555))


def _slope(hd):
    return 2.0 ** (-8.0 * (hd + 1) / N_HEADS)


def _sparse_attn_kernel(q_ref, qs_ref, qi_ref, wi_ref, kt_ref, va_ref, kit_ref, o_ref,
                        key_ref, plane_ref, act_ref, qa_ref, sa_ref, sb_ref, m_ref, acc_ref,
                        *, tq, ck, S, pos0, topk):
    tpc = ck // LANES
    cpg = PLANE_BITS // tpc
    rgrp = tq // SUBLANES
    qb = pl.program_id(1)
    p_first = pos0 + qb * tq
    limit = jnp.minimum(S, ((p_first + tq - 1) // CHUNK + 1) * CHUNK)
    nkc = (limit + ck - 1) // ck
    ngrp = (nkc + cpg - 1) // cpg
    qpos = p_first + lax.broadcasted_iota(jnp.int32, (tq, 1), 0)
    qchunk = qpos // CHUNK
    lane_pos = lax.broadcasted_iota(jnp.int32, (tq, ck), 1)
    wi = wi_ref[0]
    qi = qi_ref[0]
    qih = [qi[:, hd * IDX_DIM:(hd + 1) * IDX_DIM] for hd in range(IDX_HEADS)]

    for hd in range(N_HEADS):
        n, g = divmod(hd, GQA_G)
        qa_ref[n, g * tq:(g + 1) * tq, 0:HEAD_DIM] = q_ref[0, :, hd * HEAD_DIM:(hd + 1) * HEAD_DIM]
        qa_ref[n, g * tq:(g + 1) * tq, HEAD_DIM:2 * HEAD_DIM] = qs_ref[hd]

    def score_body(c, carry):
        kic = kit_ref[0, c]
        sc = jnp.zeros((tq, ck), f32)
        for hd in range(IDX_HEADS):
            sc = sc + wi[:, hd:hd + 1] * jnp.maximum(_mm(qih[hd], kic), 0.0)
        kpos = c * ck + lane_pos
        vis = (kpos // CHUNK <= qchunk) & (kpos < S)
        bits = pltpu.bitcast(sc, jnp.int32)
        key = bits ^ ((bits >> 31) & 0x7FFFFFFF)
        key_ref[c] = jnp.where(vis, key, INT_MIN)
        return carry

    lax.fori_loop(0, nkc, score_body, 0)

    def fill_body(c, carry):
        key_ref[c] = jnp.full((tq, ck), INT_MIN, jnp.int32)
        return carry

    lax.fori_loop(nkc, ngrp * cpg, fill_body, 0)

    def xpose_body(it, carry):
        g = it // rgrp
        rows = pl.ds(pl.multiple_of((it % rgrp) * SUBLANES, SUBLANES), SUBLANES)
        w = [key_ref[g * cpg + t // tpc, rows, (t % tpc) * LANES:(t % tpc + 1) * LANES]
             for t in range(PLANE_BITS)]
        for j, msk in BUTTERFLY:
            for k in range(PLANE_BITS):
                if k & j == 0:
                    tmp = (w[k] ^ lax.shift_right_logical(w[k + j], jnp.int32(j))) & msk
                    w[k] = w[k] ^ tmp
                    w[k + j] = w[k + j] ^ (tmp << j)
        plane_ref[g, 0, rows, :] = ~w[0]
        for i in range(1, PLANE_BITS):
            plane_ref[g, i, rows, :] = w[i]
        act_ref[g, rows, :] = jnp.full((SUBLANES, LANES), -1, jnp.int32)
        return carry

    lax.fori_loop(0, ngrp * rgrp, xpose_body, 0)

    def pair_body(i, carry):
        need, thr_u = carry

        def gsum(g, parts):
            a, p1, p2 = act_ref[g], plane_ref[g, 2 * i], plane_ref[g, 2 * i + 1]
            a1, a0 = a & p1, a & ~p1
            return (parts[0] + lax.population_count(a1 & p2), parts[1] + lax.population_count(a1 & ~p2),
                    parts[2] + lax.population_count(a0 & p2))

        zero = jnp.zeros((tq, LANES), jnp.int32)
        parts = lax.fori_loop(0, ngrp, gsum, (zero, zero, zero))
        cnt = jnp.sum(jnp.concatenate(parts, axis=0), axis=1, keepdims=True)
        t1 = cnt[0:tq]
        t2 = t1 + cnt[tq:2 * tq]
        t3 = t2 + cnt[2 * tq:3 * tq]
        hi = need <= t2
        lo = (need <= t1) | (jnp.logical_not(hi) & (need <= t3))
        flip1 = jnp.where(hi, 0, -1)
        flip2 = jnp.where(lo, 0, -1)

        def gupd(g, c_):
            act_ref[g] = act_ref[g] & (plane_ref[g, 2 * i] ^ flip1) & (plane_ref[g, 2 * i + 1] ^ flip2)
            return c_

        lax.fori_loop(0, ngrp, gupd, 0)
        need = need - jnp.where(hi, jnp.where(lo, 0, t1), jnp.where(lo, t2, t3))
        bits = jnp.where(hi, 2, 0) | jnp.where(lo, 1, 0)
        return need, thr_u | lax.shift_left(bits, 30 - 2 * i)

    need, thr_u = lax.fori_loop(0, PLANE_BITS // 2, pair_body,
                                (jnp.full((tq, 1), topk, jnp.int32), jnp.zeros((tq, 1), jnp.int32)))
    thr = thr_u ^ INT_MIN
    n_eq = jnp.sum(lax.fori_loop(0, ngrp, lambda g, part: part + lax.population_count(act_ref[g]),
                                 jnp.zeros((tq, LANES), jnp.int32)), axis=1, keepdims=True)

    def count(pred):
        def body(c, part):
            hit = pred(key_ref[c], c)
            for s in range(tpc):
                part = part + hit[:, s * LANES:(s + 1) * LANES].astype(jnp.int32)
            return part
        part = lax.fori_loop(0, nkc, body, jnp.zeros((tq, LANES), jnp.int32))
        return jnp.sum(part, axis=1, keepdims=True)

    real = thr != INT_MIN
    excess = real & (n_eq > need)
    pos_bits = max(1, (ck * key_ref.shape[0] - 1).bit_length())

    def tie_search():
        def pbody(i, ans):
            cand = ans | lax.shift_left(jnp.int32(1), pos_bits - 1 - i)
            cnt = count(lambda kc, c: (kc == thr) & (c * ck + lane_pos < cand))
            return jnp.where(cnt < need, cand, ans)
        return lax.fori_loop(0, pos_bits, pbody, jnp.zeros((tq, 1), jnp.int32))

    any_excess = jnp.max(excess.astype(jnp.int32)) > 0
    tie_lim = lax.cond(any_excess, tie_search, lambda: jnp.zeros((tq, 1), jnp.int32))
    big = jnp.int32(ck * key_ref.shape[0])
    tie_lim = jnp.where(excess, tie_lim, jnp.where(real, big, -1))

    m_ref[...] = jnp.full(m_ref.shape, NEG, f32)
    acc_ref[...] = jnp.zeros(acc_ref.shape, f32)

    def qk_chunk(c, s_ref):
        for n in range(N_KV):
            s_ref[n] = _mm(qa_ref[n], kt_ref[0, c, n])

    def attn_chunk(c, s_ref, last):
        start = pl.multiple_of(c * ck, ck)
        kc = key_ref[c]
        kpos = start + lane_pos
        sel = (kc > thr) | ((kc == thr) & (kpos <= tie_lim))
        bias = jnp.where(sel, 0.0, NEG)
        if last:
            ahead = jnp.maximum(kpos - qpos, 0).astype(f32)
        for n in range(N_KV):
            ps, alphas = [], []
            for g in range(GQA_G):
                rows = slice(g * tq, (g + 1) * tq)
                sg = s_ref[n, rows, :] + bias
                if last:
                    sg = sg - (2.0 * _slope(n * GQA_G + g)) * ahead
                m_prev = m_ref[n, rows]
                m_new = jnp.maximum(m_prev, jnp.max(sg, axis=1, keepdims=True))
                m_ref[n, rows] = m_new
                alphas.append(jnp.exp(m_prev - m_new))
                ps.append(jnp.exp(sg - m_new).astype(bf16))
            pv = _mm(jnp.concatenate(ps, axis=0),
                     va_ref[0, pl.ds(start, ck), n * 2 * HEAD_DIM:(n + 1) * 2 * HEAD_DIM])
            acc_ref[n] = jnp.concatenate(alphas, axis=0) * acc_ref[n] + pv

    n_rest = nkc - 1
    qk_chunk(n_rest, sa_ref)
    qk_chunk(0, sb_ref)
    attn_chunk(n_rest, sa_ref, True)

    def attn_body(i, carry):
        qk_chunk(2 * i + 1, sa_ref)
        attn_chunk(2 * i, sb_ref, False)
        qk_chunk(2 * i + 2, sb_ref)
        attn_chunk(2 * i + 1, sa_ref, False)
        return carry

    lax.fori_loop(0, n_rest // 2, attn_body, 0)

    @pl.when(n_rest % 2 == 1)
    def _():
        attn_chunk(n_rest - 1, sb_ref, False)

    for hd in range(N_HEADS):
        n, g = divmod(hd, GQA_G)
        a = acc_ref[n, g * tq:(g + 1) * tq, :]
        o_ref[0, :, hd * HEAD_DIM:(hd + 1) * HEAD_DIM] = (a[:, :HEAD_DIM] / a[:, HEAD_DIM:]).astype(bf16)


def _sparse_attn(q, qi, wi, k_all, v_all, ki_all, pos0, tq, ck, topk):
    B, T, _ = q.shape
    S = k_all.shape[1]
    nc = -(-S // ck)
    Sp = nc * ck
    tpc = ck // LANES
    assert ck % LANES == 0 and PLANE_BITS % tpc == 0 and T % tq == 0
    cpg = PLANE_BITS // tpc
    n_grp = -(-nc // cpg)
    assert Sp <= 256 * LANES
    for qb in range(T // tq):
        p_first = pos0 + qb * tq
        limit = min(S, ((p_first + tq - 1) // CHUNK + 1) * CHUNK)
        assert p_first >= (-(-limit // ck) - 1) * ck
    pad = lambda a: jnp.pad(a.astype(bf16), ((0, 0), (0, Sp - S), (0, 0)))
    j = jnp.arange(Sp)
    pos_rows = jnp.zeros((HEAD_DIM, Sp), f32).at[0].set(j // LANES).at[1].set(j % LANES).astype(bf16)
    kt = pad(k_all).reshape(B, Sp, N_KV, HEAD_DIM).transpose(0, 2, 3, 1)
    kt = jnp.concatenate([kt, jnp.broadcast_to(pos_rows, kt.shape)], axis=2)
    kt = kt.reshape(B, N_KV, 2 * HEAD_DIM, nc, ck).transpose(0, 3, 1, 2, 4)
    va = pad(v_all).reshape(B, Sp, N_KV, HEAD_DIM)
    va = jnp.concatenate([va, jnp.ones_like(va)], axis=-1).reshape(B, Sp, N_KV * 2 * HEAD_DIM)
    kit = pad(ki_all).transpose(0, 2, 1).reshape(B, IDX_DIM, nc, ck).transpose(0, 2, 1, 3)
    assert all((8.0 * (hd + 1) / N_HEADS).is_integer() for hd in range(N_HEADS))
    slopes = jnp.asarray([_slope(hd) for hd in range(N_HEADS)], f32)
    qs = jnp.zeros((N_HEADS, tq, HEAD_DIM), f32).at[:, :, 0].set(LANES * slopes[:, None])
    qs = qs.at[:, :, 1].set(slopes[:, None]).astype(bf16)

    kern = functools.partial(_sparse_attn_kernel, tq=tq, ck=ck, S=S, pos0=pos0, topk=topk)
    hq = N_HEADS * HEAD_DIM
    once = pl.Buffered(1)
    return pl.pallas_call(
        kern,
        grid=(B, T // tq),
        in_specs=[pl.BlockSpec((1, tq, hq), lambda b, t: (b, t, 0)),
                  pl.BlockSpec((N_HEADS, tq, HEAD_DIM), lambda b, t: (0, 0, 0), pipeline_mode=once),
                  pl.BlockSpec((1, tq, IDX_HEADS * IDX_DIM), lambda b, t: (b, t, 0)),
                  pl.BlockSpec((1, tq, IDX_HEADS), lambda b, t: (b, t, 0)),
                  pl.BlockSpec((1, nc, N_KV, 2 * HEAD_DIM, ck), lambda b, t: (b, 0, 0, 0, 0), pipeline_mode=once),
                  pl.BlockSpec((1, Sp, N_KV * 2 * HEAD_DIM), lambda b, t: (b, 0, 0), pipeline_mode=once),
                  pl.BlockSpec((1, nc, IDX_DIM, ck), lambda b, t: (b, 0, 0, 0), pipeline_mode=once)],
        out_specs=pl.BlockSpec((1, tq, hq), lambda b, t: (b, t, 0)),
        out_shape=jax.ShapeDtypeStruct((B, T, hq), bf16),
        scratch_shapes=[pltpu.VMEM((n_grp * cpg, tq, ck), jnp.int32),
                        pltpu.VMEM((n_grp, PLANE_BITS, tq, LANES), jnp.int32),
                        pltpu.VMEM((n_grp, tq, LANES), jnp.int32),
                        pltpu.VMEM((N_KV, GQA_G * tq, 2 * HEAD_DIM), bf16),
                        pltpu.VMEM((N_KV, GQA_G * tq, ck), f32),
                        pltpu.VMEM((N_KV, GQA_G * tq, ck), f32),
                        pltpu.VMEM((N_KV, GQA_G * tq, 1), f32),
                        pltpu.VMEM((N_KV, GQA_G * tq, 2 * HEAD_DIM), f32)],
        compiler_params=_cparams(2),
    )(q, qs, qi, wi, kt, va, kit)


def _merge_kernel(x_ref, h_ref, ya_ref, yb_ref, o_ref, g1_ref, wgl_ref, wpc_ref, wo_ref,
                  gn2_ref, sc2_ref, sh2_ref, x1_ref, h2_ref, *, bb, tt):
    D = D_MODEL
    M = bb * tt
    h = h_ref[...].reshape(M, D)
    yc = _mm(o_ref[...].reshape(M, D), wpc_ref[...])
    m = _sigmoid(_mm(h, wgl_ref[:, 0:D])) * ya_ref[...].reshape(M, D).astype(f32)
    m = m + _sigmoid(_mm(h, wgl_ref[:, D:2 * D])) * yb_ref[...].reshape(M, D).astype(f32)
    m = m + _sigmoid(_mm(h, wgl_ref[:, 2 * D:3 * D])) * yc
    x1 = x_ref[...] + g1_ref[...] * _mm(m.astype(bf16), wo_ref[...]).reshape(bb, tt, D)
    x1_ref[...] = x1
    h2 = _rms(x1, gn2_ref[...]) * (1.0 + sc2_ref[...]) + sh2_ref[...]
    h2_ref[...] = h2.astype(bf16)


def _merge(x, h, ya, yb, o, g1, wgl, wpc, wo, gn2, sc2, sh2):
    B, T, D = x.shape
    bb, tt = _tiles(B, T)
    kern = functools.partial(_merge_kernel, bb=bb, tt=tt)
    row = _row_spec(bb, tt, D)
    vec = _batch_spec(bb, 1, D)
    return pl.pallas_call(
        kern,
        grid=(B // bb, T // tt),
        in_specs=[row, row, row, row, row, vec, _const_spec((D, 3 * D)), _const_spec((D, D)),
                  _const_spec((D, D)), _const_spec((1, 1, D)), vec, vec],
        out_specs=[row, row],
        out_shape=[jax.ShapeDtypeStruct((B, T, D), f32), jax.ShapeDtypeStruct((B, T, D), bf16)],
        compiler_params=_cparams(2),
    )(x, h, ya, yb, o, g1, wgl, wpc, wo, gn2.reshape(1, 1, D), sc2, sh2)


FF_TILE = 1024


def _mlp_kernel(x1_ref, h2_ref, g2_ref, w1_ref, b1_ref, w2_ref, b2_ref, x2_ref, *, bb, tt):
    D = D_MODEL
    M = bb * tt
    h2 = h2_ref[...].reshape(M, D)
    acc = jnp.zeros((M, D), f32)
    for c0 in range(0, D_FF, FF_TILE):
        f = jnp.maximum(_mm(h2, w1_ref[:, c0:c0 + FF_TILE]) + b1_ref[:, c0:c0 + FF_TILE], 0.0)
        acc = acc + _mm((f * f).astype(bf16), w2_ref[c0:c0 + FF_TILE, :])
    f = (acc + b2_ref[...]).reshape(bb, tt, D)
    x2_ref[...] = x1_ref[...] + g2_ref[...] * f


def _mlp(x1, h2, g2, w1, b1, w2, b2):
    B, T, D = x1.shape
    bb, tt = _tiles(B, T)
    kern = functools.partial(_mlp_kernel, bb=bb, tt=tt)
    row = _row_spec(bb, tt, D)
    return pl.pallas_call(
        kern,
        grid=(B // bb, T // tt),
        in_specs=[row, row, _batch_spec(bb, 1, D), _const_spec((D, D_FF)), _const_spec((1, D_FF)),
                  _const_spec((D_FF, D)), _const_spec((1, D))],
        out_specs=row,
        out_shape=jax.ShapeDtypeStruct((B, T, D), f32),
        compiler_params=_cparams(2),
    )(x1, h2, g2, w1, b1.reshape(1, D_FF), w2, b2.reshape(1, D))


def _split_w_in(w_in):
    D = D_MODEL
    w = w_in.astype(bf16)
    o_b = 2 * D
    o_q = 4 * D
    o_gl = o_q + N_HEADS * HEAD_DIM + 2 * N_KV * HEAD_DIM + IDX_HEADS * IDX_DIM + IDX_DIM + IDX_HEADS
    n_c = o_gl - o_q
    w_c = jnp.pad(w[:, o_q:o_gl], ((0, 0), (0, QKV_COLS - n_c)))
    return dict(wa1=w[:, 0:D], wa2=w[:, D:2 * D], wbu=w[:, o_b:o_b + D], wbv=w[:, o_b + D:o_b + 2 * D],
                w_c=w_c, wgl=w[:, o_gl:o_gl + 3 * D])


def _layer(x, mod, cache, W, pos0):
    B, T, D = x.shape
    sh1, sc1, g1, sh2, sc2, g2 = mod
    win = W['w_in']
    h = _prenorm(x, W['g_norm1'], sc1, sh1)

    hist = jnp.zeros((B, HIST, D), f32) if cache is None else cache[3]
    ya, new_conv = _branch_conv(h, hist, win['wa1'], win['wa2'], W['w_dw'], W['b_dw'],
                                W['ln_c_g'], W['ln_c_b'], W['w_pa'])

    L = GMLP_CHUNK if cache is None else T
    res = _branch_gmlp(h, win['wbu'], win['wbv'], W['ln_v_g'], W['ln_v_b'], W['w_s'], W['b_s'],
                       W['w_pb'], L, cache is not None)
    yb = res[0]
    vg = res[1] if cache is not None else None

    q, k, v, qi, ki, wi = _attn_proj(h, win['w_c'], W['g_q'], W['g_k'])
    if cache is None:
        k_all, v_all, ki_all = k, v, ki
        ck = 512
    else:
        k_all = jnp.concatenate([cache[0].reshape(B, -1, N_KV * HEAD_DIM), k], axis=1)
        v_all = jnp.concatenate([cache[1].reshape(B, -1, N_KV * HEAD_DIM), v], axis=1)
        ki_all = jnp.concatenate([cache[2], ki], axis=1)
        ck = 256
    topk = min(TOPK_MAX, k_all.shape[1] // 4)
    o = _sparse_attn(q, qi, wi, k_all, v_all, ki_all, pos0, min(Q_BLOCK, T), ck, topk)

    x1, h2 = _merge(x, h, ya, yb, o, g1, win['wgl'], W['w_pc'], W['w_o'], W['g_norm2'], sc2, sh2)
    x2 = _mlp(x1, h2, g2, W['w_1'], W['b_1'], W['w_2'], W['b_2'])
    return x2, (k.reshape(B, T, N_KV, HEAD_DIM), v.reshape(B, T, N_KV, HEAD_DIM), ki, new_conv, vg)


def kernel(x_prompt, x_sample, c_prompt, c_sample, cache_k, cache_v, cache_kidx, cache_conv, w_ada, b_ada, g_norm1, w_in, w_dw, b_dw, ln_c_g, ln_c_b, w_pa, ln_v_g, ln_v_b, w_s, b_s, w_pb, g_q, g_k, w_pc, w_o, g_norm2, w_1, b_1, w_2, b_2):
    depth = w_in.shape[0]
    D = D_MODEL
    Bp, Bs = x_prompt.shape[0], x_sample.shape[0]
    past = cache_k.shape[2]
    c_all = jnp.concatenate([c_prompt, c_sample], axis=0)
    n_c = c_all.shape[0]
    n_pad = -(-n_c // SUBLANES) * SUBLANES
    mod_all = _ada(jnp.pad(c_all, ((0, n_pad - n_c), (0, 0))), w_ada.astype(bf16), b_ada)

    def mods(l, lo, n):
        m = mod_all[l, lo:lo + n]
        return [m[:, None, i * D:(i + 1) * D] for i in range(6)]

    hp, hs = x_prompt, x_sample
    outs_p, outs_s = [], []
    for l in range(depth):
        W = dict(g_norm1=g_norm1[l], w_in=_split_w_in(w_in[l]), w_dw=w_dw[l], b_dw=b_dw[l],
                 ln_c_g=ln_c_g[l], ln_c_b=ln_c_b[l], w_pa=w_pa[l].astype(bf16), ln_v_g=ln_v_g[l],
                 ln_v_b=ln_v_b[l], w_s=w_s[l], b_s=b_s[l], w_pb=w_pb[l].astype(bf16), g_q=g_q[l],
                 g_k=g_k[l], w_pc=w_pc[l].astype(bf16), w_o=w_o[l].astype(bf16), g_norm2=g_norm2[l],
                 w_1=w_1[l].astype(bf16), b_1=b_1[l], w_2=w_2[l].astype(bf16), b_2=b_2[l])
        hp, op = _layer(hp, mods(l, 0, Bp), None, W, 0)
        hs, os_ = _layer(hs, mods(l, Bp, Bs), (cache_k[l], cache_v[l], cache_kidx[l], cache_conv[l]), W, past)
        outs_p.append(op)
        outs_s.append(os_)
    stack = lambda outs, i: jnp.stack([o[i] for o in outs])
    return (hp, hs,
            stack(outs_p, 0), stack(outs_p, 1), stack(outs_p, 2), stack(outs_p, 3),
            stack(outs_s, 0), stack(outs_s, 1), stack(outs_s, 2), stack(outs_s, 3), stack(outs_s, 4))
```

```python
import functools

import jax
import jax.numpy as jnp
from jax import lax
from jax.experimental import pallas as pl
from jax.experimental.pallas import tpu as pltpu

D_MODEL = 1024
CHUNK = 64
CONV_W = 31
HIST = CONV_W - 1
GMLP_CHUNK = 128
GMLP_GROUPS = 8
GMLP_GW = D_MODEL // GMLP_GROUPS
N_HEADS = 8
HEAD_DIM = 128
N_KV = 2
GQA_G = N_HEADS // N_KV
IDX_HEADS = 8
IDX_DIM = 64
TOPK_MAX = 256
Q_BLOCK = 128
D_FF = 4 * D_MODEL
EPS = 1e-6

LANES = 128
SUBLANES = 8
HALO = 32
ROW_TILE = 512
VMEM_LIMIT = 56 * 1024 * 1024
INT_MIN = -2 ** 31
NEG = -1e30

f32 = jnp.float32
bf16 = jnp.bfloat16


def _cparams(n_axes):
    return pltpu.CompilerParams(dimension_semantics=("arbitrary",) * n_axes,
                                vmem_limit_bytes=VMEM_LIMIT)


def _tiles(B, T):
    if T >= ROW_TILE:
        assert T % ROW_TILE == 0
        return 1, ROW_TILE
    assert ROW_TILE % T == 0 and B % (ROW_TILE // T) == 0
    return ROW_TILE // T, T


def _row_spec(bb, tt, n):
    return pl.BlockSpec((bb, tt, n), lambda b, t: (b, t, 0))


def _batch_spec(bb, r, n):
    return pl.BlockSpec((bb, r, n), lambda b, t: (b, 0, 0))


def _const_spec(shape):
    nd = len(shape)
    return pl.BlockSpec(shape, lambda b, t: (0,) * nd, pipeline_mode=pl.Buffered(1))


def _mm(a, w):
    return jnp.dot(a, w, preferred_element_type=f32)


def _sigmoid(x):
    return 1.0 / (1.0 + jnp.exp(-x))


def _silu(x):
    return x * _sigmoid(x)


def _gelu_tanh(x):
    return 0.5 * x * (1.0 + jnp.tanh(0.7978845608028654 * (x + 0.044715 * (x * x * x))))


def _rms(x, g):
    return x * lax.rsqrt(jnp.mean(x * x, axis=-1, keepdims=True) + EPS) * g


def _ln(x, g, b):
    mu = jnp.mean(x, axis=-1, keepdims=True)
    xc = x - mu
    var = jnp.mean(xc * xc, axis=-1, keepdims=True)
    return xc * lax.rsqrt(var + EPS) * g + b


def _ada_kernel(c_ref, w_ref, b_ref, o_ref):
    c = c_ref[...]
    o_ref[0] = _mm(_silu(c).astype(bf16), w_ref[0]) + b_ref[0]


def _ada(c_all, w_ada, b_ada):
    depth, d, n = w_ada.shape
    bp = c_all.shape[0]
    tn = 1024
    return pl.pallas_call(
        _ada_kernel,
        grid=(depth, n // tn),
        in_specs=[pl.BlockSpec((bp, d), lambda l, j: (0, 0)),
                  pl.BlockSpec((1, d, tn), lambda l, j: (l, 0, j)),
                  pl.BlockSpec((1, 1, tn), lambda l, j: (l, 0, j))],
        out_specs=pl.BlockSpec((1, bp, tn), lambda l, j: (l, 0, j)),
        out_shape=jax.ShapeDtypeStruct((depth, bp, n), f32),
        compiler_params=_cparams(2),
    )(c_all, w_ada, b_ada.reshape(depth, 1, n))


def _prenorm_kernel(x_ref, g_ref, sc_ref, sh_ref, h_ref):
    h = _rms(x_ref[...], g_ref[...]) * (1.0 + sc_ref[...]) + sh_ref[...]
    h_ref[...] = h.astype(bf16)


def _prenorm(x, g, sc, sh):
    B, T, D = x.shape
    bb, tt = _tiles(B, T)
    return pl.pallas_call(
        _prenorm_kernel,
        grid=(B // bb, T // tt),
        in_specs=[_row_spec(bb, tt, D), _const_spec((1, 1, D)),
                  _batch_spec(bb, 1, D), _batch_spec(bb, 1, D)],
        out_specs=_row_spec(bb, tt, D),
        out_shape=jax.ShapeDtypeStruct((B, T, D), bf16),
        compiler_params=_cparams(2),
    )(x, g.reshape(1, 1, D), sc, sh)


CONV_ROWS = 64
CONV_COLS = 256


def _conv_kernel(h_ref, hist_ref, wa1_ref, wa2_ref, wdw_ref, bdw_ref, lng_ref, lnb_ref, wpa_ref,
                 ya_ref, newc_ref, apad_ref, conv_ref, shift_ref, *, bb, tt):
    D = D_MODEL
    t = pl.program_id(1)

    @pl.when(t == 0)
    def _():
        apad_ref[:, 0:HALO - HIST, :] = jnp.zeros((bb, HALO - HIST, D), f32)
        apad_ref[:, HALO - HIST:HALO, :] = hist_ref[...]

    h = h_ref[...].reshape(bb * tt, D)
    a = _mm(h, wa1_ref[...]) * _sigmoid(_mm(h, wa2_ref[...]))
    apad_ref[:, HALO:HALO + tt, :] = a.reshape(bb, tt, D)

    base = HALO - HIST
    n_sh = HALO + tt - SUBLANES
    for b in range(bb):
        for c0 in range(0, D, CONV_COLS):
            cols = slice(c0, c0 + CONV_COLS)
            for s in range(1, SUBLANES):
                shift_ref[s - 1, 0:n_sh, :] = apad_ref[b, s:s + n_sh, cols]
            for r0 in range(0, tt, CONV_ROWS):
                acc = jnp.zeros((CONV_ROWS, CONV_COLS), f32)
                for j in range(CONV_W):
                    q8, s = divmod(base + j, SUBLANES)
                    lo = r0 + SUBLANES * q8
                    src = (apad_ref[b, lo:lo + CONV_ROWS, cols] if s == 0
                           else shift_ref[s - 1, lo:lo + CONV_ROWS, :])
                    acc = acc + wdw_ref[j:j + 1, cols] * src
                conv_ref[b * tt + r0:b * tt + r0 + CONV_ROWS, cols] = acc

    tail = apad_ref[:, HALO + tt - HIST:HALO + tt, :]
    newc_ref[...] = tail
    apad_ref[:, HALO - HIST:HALO, :] = tail

    y = conv_ref[...] + bdw_ref[...]
    z = _silu(_ln(y, lng_ref[...], lnb_ref[...]))
    ya_ref[...] = _mm(z.astype(bf16), wpa_ref[...]).astype(bf16).reshape(bb, tt, D)


def _branch_conv(h, hist, wa1, wa2, wdw, bdw, lng, lnb, wpa):
    B, T, D = h.shape
    bb, tt = _tiles(B, T)
    kern = functools.partial(_conv_kernel, bb=bb, tt=tt)
    return pl.pallas_call(
        kern,
        grid=(B // bb, T // tt),
        in_specs=[_row_spec(bb, tt, D), _batch_spec(bb, HIST, D),
                  _const_spec((D, D)), _const_spec((D, D)), _const_spec((CONV_W, D)),
                  _const_spec((1, D)), _const_spec((1, D)), _const_spec((1, D)), _const_spec((D, D))],
        out_specs=[_row_spec(bb, tt, D), _batch_spec(bb, HIST, D)],
        out_shape=[jax.ShapeDtypeStruct((B, T, D), bf16), jax.ShapeDtypeStruct((B, HIST, D), f32)],
        scratch_shapes=[pltpu.VMEM((bb, HALO + tt, D), f32), pltpu.VMEM((bb * tt, D), f32),
                        pltpu.VMEM((SUBLANES - 1, HALO + tt, CONV_COLS), f32)],
        compiler_params=_cparams(2),
    )(h, hist, wa1, wa2, wdw, bdw.reshape(1, D), lng.reshape(1, D), lnb.reshape(1, D), wpa)


def _gmlp_kernel(h_ref, wbu_ref, wbv_ref, lng_ref, lnb_ref, wm_ref, bs_ref, wpb_ref,
                 yb_ref, *rest, bb, tt, L, emit_v):
    D = D_MODEL
    if emit_v:
        vg_ref, s_ref = rest
    else:
        (s_ref,) = rest
    M = bb * tt
    h = h_ref[...].reshape(M, D)
    u = _gelu_tanh(_mm(h, wbu_ref[...]))
    vg = _ln(_gelu_tanh(_mm(h, wbv_ref[...])), lng_ref[...], lnb_ref[...])
    if emit_v:
        vg_ref[...] = vg.reshape(bb, tt, D)
    vgb = vg.astype(bf16)
    for c in range(M // L):
        for g in range(GMLP_GROUPS):
            blk = vgb[c * L:(c + 1) * L, g * GMLP_GW:(g + 1) * GMLP_GW]
            s_ref[c * L:(c + 1) * L, g * GMLP_GW:(g + 1) * GMLP_GW] = _mm(wm_ref[g], blk) + bs_ref[:, g * GMLP_GW:(g + 1) * GMLP_GW]
    yb_ref[...] = _mm((u * s_ref[...]).astype(bf16), wpb_ref[...]).astype(bf16).reshape(bb, tt, D)


def _branch_gmlp(h, wbu, wbv, lng, lnb, w_s, b_s, wpb, L, emit_v):
    B, T, D = h.shape
    bb, tt = _tiles(B, T)
    assert tt % L == 0
    tril = jnp.tril(jnp.ones((L, L), dtype=bool))
    wm = jnp.where(tril[None], w_s[:, :L, :L], 0).astype(bf16)
    bs = jnp.repeat(jnp.swapaxes(b_s[:, :L], 0, 1), GMLP_GW, axis=1)
    kern = functools.partial(_gmlp_kernel, bb=bb, tt=tt, L=L, emit_v=emit_v)
    out_specs = [_row_spec(bb, tt, D)]
    out_shape = [jax.ShapeDtypeStruct((B, T, D), bf16)]
    if emit_v:
        out_specs.append(_row_spec(bb, tt, D))
        out_shape.append(jax.ShapeDtypeStruct((B, T, D), f32))
    return pl.pallas_call(
        kern,
        grid=(B // bb, T // tt),
        in_specs=[_row_spec(bb, tt, D), _const_spec((D, D)), _const_spec((D, D)),
                  _const_spec((1, D)), _const_spec((1, D)), _const_spec((GMLP_GROUPS, L, L)),
                  _const_spec((L, D)), _const_spec((D, D))],
        out_specs=out_specs,
        out_shape=out_shape,
        scratch_shapes=[pltpu.VMEM((bb * tt, D), f32)],
        compiler_params=_cparams(2),
    )(h, wbu, wbv, lng.reshape(1, D), lnb.reshape(1, D), wm, bs, wpb)


QKV_COLS = N_HEADS * HEAD_DIM + 2 * N_KV * HEAD_DIM + IDX_HEADS * IDX_DIM + LANES


def _attn_proj_kernel(h_ref, w_ref, gq_ref, gk_ref, q_ref, k_ref, v_ref, qi_ref, ki_ref, wi_ref, *, bb, tt):
    D = D_MODEL
    M = bb * tt
    p = _mm(h_ref[...].reshape(M, D), w_ref[...])
    o = 0
    for hd in range(N_HEADS):
        q_ref[:, :, hd * HEAD_DIM:(hd + 1) * HEAD_DIM] = (_rms(
            p[:, o:o + HEAD_DIM], gq_ref[...]) * HEAD_DIM ** -0.5).astype(bf16).reshape(bb, tt, HEAD_DIM)
        o += HEAD_DIM
    for n in range(N_KV):
        k_ref[:, :, n * HEAD_DIM:(n + 1) * HEAD_DIM] = _rms(
            p[:, o:o + HEAD_DIM], gk_ref[...]).reshape(bb, tt, HEAD_DIM)
        o += HEAD_DIM
    v_ref[...] = p[:, o:o + N_KV * HEAD_DIM].reshape(bb, tt, N_KV * HEAD_DIM)
    o += N_KV * HEAD_DIM
    qi_ref[...] = p[:, o:o + IDX_HEADS * IDX_DIM].astype(bf16).reshape(bb, tt, IDX_HEADS * IDX_DIM)
    o += IDX_HEADS * IDX_DIM
    ki_ref[...] = p[:, o:o + IDX_DIM].reshape(bb, tt, IDX_DIM)
    o += IDX_DIM
    wi_ref[...] = (p[:, o:o + IDX_HEADS] * IDX_HEADS ** -0.5 * IDX_DIM ** -0.5).reshape(bb, tt, IDX_HEADS)


def _attn_proj(h, w_c, g_q, g_k):
    B, T, D = h.shape
    bb, tt = _tiles(B, T)
    kern = functools.partial(_attn_proj_kernel, bb=bb, tt=tt)
    widths = (N_HEADS * HEAD_DIM, N_KV * HEAD_DIM, N_KV * HEAD_DIM, IDX_HEADS * IDX_DIM, IDX_DIM, IDX_HEADS)
    dtypes = (bf16, f32, f32, bf16, f32, f32)
    return pl.pallas_call(
        kern,
        grid=(B // bb, T // tt),
        in_specs=[_row_spec(bb, tt, D), _const_spec((D, QKV_COLS)),
                  _const_spec((1, HEAD_DIM)), _const_spec((1, HEAD_DIM))],
        out_specs=[_row_spec(bb, tt, n) for n in widths],
        out_shape=[jax.ShapeDtypeStruct((B, T, n), dt) for n, dt in zip(widths, dtypes)],
        compiler_params=_cparams(2),
    )(h, w_c, g_q.reshape(1, HEAD_DIM), g_k.reshape(1, HEAD_DIM))


PLANE_BITS = 32
BUTTERFLY = ((16, 0x0000FFFF), (8, 0x00FF00FF), (4, 0x0F0F0F0F), (2, 0x33333333), (1, 0x55555555))


def _slope(hd):
    return 2.0 ** (-8.0 * (hd + 1) / N_HEADS)


def _sparse_attn_kernel(q_ref, qs_ref, qi_ref, wi_ref, kt_ref, va_ref, kit_ref, o_ref,
                        key_ref, plane_ref, act_ref, qa_ref, sa_ref, sb_ref, m_ref, acc_ref,
                        *, tq, ck, S, pos0, topk):
    tpc = ck // LANES
    cpg = PLANE_BITS // tpc
    rgrp = tq // SUBLANES
    qb = pl.program_id(1)
    p_first = pos0 + qb * tq
    limit = jnp.minimum(S, ((p_first + tq - 1) // CHUNK + 1) * CHUNK)
    nkc = (limit + ck - 1) // ck
    ngrp = (nkc + cpg - 1) // cpg
    qpos = p_first + lax.broadcasted_iota(jnp.int32, (tq, 1), 0)
    qchunk = qpos // CHUNK
    lane_pos = lax.broadcasted_iota(jnp.int32, (tq, ck), 1)
    wi = wi_ref[0]
    qi = qi_ref[0]
    qih = [qi[:, hd * IDX_DIM:(hd + 1) * IDX_DIM] for hd in range(IDX_HEADS)]

    for hd in range(N_HEADS):
        n, g = divmod(hd, GQA_G)
        qa_ref[n, g * tq:(g + 1) * tq, 0:HEAD_DIM] = q_ref[0, :, hd * HEAD_DIM:(hd + 1) * HEAD_DIM]
        qa_ref[n, g * tq:(g + 1) * tq, HEAD_DIM:2 * HEAD_DIM] = qs_ref[hd]

    def score_body(c, carry):
        kic = kit_ref[0, c]
        sc = jnp.zeros((tq, ck), f32)
        for hd in range(IDX_HEADS):
            sc = sc + wi[:, hd:hd + 1] * jnp.maximum(_mm(qih[hd], kic), 0.0)
        kpos = c * ck + lane_pos
        vis = (kpos // CHUNK <= qchunk) & (kpos < S)
        bits = pltpu.bitcast(sc, jnp.int32)
        key = bits ^ ((bits >> 31) & 0x7FFFFFFF)
        key_ref[c] = jnp.where(vis, key, INT_MIN)
        return carry

    lax.fori_loop(0, nkc, score_body, 0)

    def fill_body(c, carry):
        key_ref[c] = jnp.full((tq, ck), INT_MIN, jnp.int32)
        return carry

    lax.fori_loop(nkc, ngrp * cpg, fill_body, 0)

    def xpose_body(it, carry):
        g = it // rgrp
        rows = pl.ds(pl.multiple_of((it % rgrp) * SUBLANES, SUBLANES), SUBLANES)
        w = [key_ref[g * cpg + t // tpc, rows, (t % tpc) * LANES:(t % tpc + 1) * LANES]
             for t in range(PLANE_BITS)]
        for j, msk in BUTTERFLY:
            for k in range(PLANE_BITS):
                if k & j == 0:
                    tmp = (w[k] ^ lax.shift_right_logical(w[k + j], jnp.int32(j))) & msk
                    w[k] = w[k] ^ tmp
                    w[k + j] = w[k + j] ^ (tmp << j)
        plane_ref[g, 0, rows, :] = ~w[0]
        for i in range(1, PLANE_BITS):
            plane_ref[g, i, rows, :] = w[i]
        act_ref[g, rows, :] = jnp.full((SUBLANES, LANES), -1, jnp.int32)
        return carry

    lax.fori_loop(0, ngrp * rgrp, xpose_body, 0)

    def pair_body(i, carry):
        need, thr_u = carry

        def gsum(g, parts):
            a, p1, p2 = act_ref[g], plane_ref[g, 2 * i], plane_ref[g, 2 * i + 1]
            a1, a0 = a & p1, a & ~p1
            return (parts[0] + lax.population_count(a1 & p2), parts[1] + lax.population_count(a1 & ~p2),
                    parts[2] + lax.population_count(a0 & p2))

        zero = jnp.zeros((tq, LANES), jnp.int32)
        parts = lax.fori_loop(0, ngrp, gsum, (zero, zero, zero))
        cnt = jnp.sum(jnp.concatenate(parts, axis=0), axis=1, keepdims=True)
        t1 = cnt[0:tq]
        t2 = t1 + cnt[tq:2 * tq]
        t3 = t2 + cnt[2 * tq:3 * tq]
        hi = need <= t2
        lo = (need <= t1) | (jnp.logical_not(hi) & (need <= t3))
        flip1 = jnp.where(hi, 0, -1)
        flip2 = jnp.where(lo, 0, -1)

        def gupd(g, c_):
            act_ref[g] = act_ref[g] & (plane_ref[g, 2 * i] ^ flip1) & (plane_ref[g, 2 * i + 1] ^ flip2)
            return c_

        lax.fori_loop(0, ngrp, gupd, 0)
        need = need - jnp.where(hi, jnp.where(lo, 0, t1), jnp.where(lo, t2, t3))
        bits = jnp.where(hi, 2, 0) | jnp.where(lo, 1, 0)
        return need, thr_u | lax.shift_left(bits, 30 - 2 * i)

    need, thr_u = lax.fori_loop(0, PLANE_BITS // 2, pair_body,
                                (jnp.full((tq, 1), topk, jnp.int32), jnp.zeros((tq, 1), jnp.int32)))
    thr = thr_u ^ INT_MIN
    n_eq = jnp.sum(lax.fori_loop(0, ngrp, lambda g, part: part + lax.population_count(act_ref[g]),
                                 jnp.zeros((tq, LANES), jnp.int32)), axis=1, keepdims=True)

    def count(pred):
        def body(c, part):
            hit = pred(key_ref[c], c)
            for s in range(tpc):
                part = part + hit[:, s * LANES:(s + 1) * LANES].astype(jnp.int32)
            return part
        part = lax.fori_loop(0, nkc, body, jnp.zeros((tq, LANES), jnp.int32))
        return jnp.sum(part, axis=1, keepdims=True)

    real = thr != INT_MIN
    excess = real & (n_eq > need)
    pos_bits = max(1, (ck * key_ref.shape[0] - 1).bit_length())

    def tie_search():
        def pbody(i, ans):
            cand = ans | lax.shift_left(jnp.int32(1), pos_bits - 1 - i)
            cnt = count(lambda kc, c: (kc == thr) & (c * ck + lane_pos < cand))
            return jnp.where(cnt < need, cand, ans)
        return lax.fori_loop(0, pos_bits, pbody, jnp.zeros((tq, 1), jnp.int32))

    any_excess = jnp.max(excess.astype(jnp.int32)) > 0
    tie_lim = lax.cond(any_excess, tie_search, lambda: jnp.zeros((tq, 1), jnp.int32))
    big = jnp.int32(ck * key_ref.shape[0])
    tie_lim = jnp.where(excess, tie_lim, jnp.where(real, big, -1))

    m_ref[...] = jnp.full(m_ref.shape, NEG, f32)
    acc_ref[...] = jnp.zeros(acc_ref.shape, f32)

    def qk_chunk(c, s_ref):
        for n in range(N_KV):
            s_ref[n] = _mm(qa_ref[n], kt_ref[0, c, n])

    def attn_chunk(c, s_ref, last):
        start = pl.multiple_of(c * ck, ck)
        kc = key_ref[c]
        kpos = start + lane_pos
        sel = (kc > thr) | ((kc == thr) & (kpos <= tie_lim))
        bias = jnp.where(sel, 0.0, NEG)
        if last:
            ahead = jnp.maximum(kpos - qpos, 0).astype(f32)
        for n in range(N_KV):
            ps, alphas = [], []
            for g in range(GQA_G):
                rows = slice(g * tq, (g + 1) * tq)
                sg = s_ref[n, rows, :] + bias
                if last:
                    sg = sg - (2.0 * _slope(n * GQA_G + g)) * ahead
                m_prev = m_ref[n, rows]
                m_new = jnp.maximum(m_prev, jnp.max(sg, axis=1, keepdims=True))
                m_ref[n, rows] = m_new
                alphas.append(jnp.exp(m_prev - m_new))
                ps.append(jnp.exp(sg - m_new).astype(bf16))
            pv = _mm(jnp.concatenate(ps, axis=0),
                     va_ref[0, pl.ds(start, ck), n * 2 * HEAD_DIM:(n + 1) * 2 * HEAD_DIM])
            acc_ref[n] = jnp.concatenate(alphas, axis=0) * acc_ref[n] + pv

    n_rest = nkc - 1
    qk_chunk(n_rest, sa_ref)
    qk_chunk(0, sb_ref)
    attn_chunk(n_rest, sa_ref, True)

    def attn_body(i, carry):
        qk_chunk(2 * i + 1, sa_ref)
        attn_chunk(2 * i, sb_ref, False)
        qk_chunk(2 * i + 2, sb_ref)
        attn_chunk(2 * i + 1, sa_ref, False)
        return carry

    lax.fori_loop(0, n_rest // 2, attn_body, 0)

    @pl.when(n_rest % 2 == 1)
    def _():
        attn_chunk(n_rest - 1, sb_ref, False)

    for hd in range(N_HEADS):
        n, g = divmod(hd, GQA_G)
        a = acc_ref[n, g * tq:(g + 1) * tq, :]
        o_ref[0, :, hd * HEAD_DIM:(hd + 1) * HEAD_DIM] = (a[:, :HEAD_DIM] / a[:, HEAD_DIM:]).astype(bf16)


def _sparse_attn(q, qi, wi, k_all, v_all, ki_all, pos0, tq, ck, topk):
    B, T, _ = q.shape
    S = k_all.shape[1]
    nc = -(-S // ck)
    Sp = nc * ck
    tpc = ck // LANES
    assert ck % LANES == 0 and PLANE_BITS % tpc == 0 and T % tq == 0
    cpg = PLANE_BITS // tpc
    n_grp = -(-nc // cpg)
    assert Sp <= 256 * LANES
    for qb in range(T // tq):
        p_first = pos0 + qb * tq
        limit = min(S, ((p_first + tq - 1) // CHUNK + 1) * CHUNK)
        assert p_first >= (-(-limit // ck) - 1) * ck
    pad = lambda a: jnp.pad(a.astype(bf16), ((0, 0), (0, Sp - S), (0, 0)))
    j = jnp.arange(Sp)
    pos_rows = jnp.zeros((HEAD_DIM, Sp), f32).at[0].set(j // LANES).at[1].set(j % LANES).astype(bf16)
    kt = pad(k_all).reshape(B, Sp, N_KV, HEAD_DIM).transpose(0, 2, 3, 1)
    kt = jnp.concatenate([kt, jnp.broadcast_to(pos_rows, kt.shape)], axis=2)
    kt = kt.reshape(B, N_KV, 2 * HEAD_DIM, nc, ck).transpose(0, 3, 1, 2, 4)
    va = pad(v_all).reshape(B, Sp, N_KV, HEAD_DIM)
    va = jnp.concatenate([va, jnp.ones_like(va)], axis=-1).reshape(B, Sp, N_KV * 2 * HEAD_DIM)
    kit = pad(ki_all).transpose(0, 2, 1).reshape(B, IDX_DIM, nc, ck).transpose(0, 2, 1, 3)
    assert all((8.0 * (hd + 1) / N_HEADS).is_integer() for hd in range(N_HEADS))
    slopes = jnp.asarray([_slope(hd) for hd in range(N_HEADS)], f32)
    qs = jnp.zeros((N_HEADS, tq, HEAD_DIM), f32).at[:, :, 0].set(LANES * slopes[:, None])
    qs = qs.at[:, :, 1].set(slopes[:, None]).astype(bf16)

    kern = functools.partial(_sparse_attn_kernel, tq=tq, ck=ck, S=S, pos0=pos0, topk=topk)
    hq = N_HEADS * HEAD_DIM
    once = pl.Buffered(1)
    return pl.pallas_call(
        kern,
        grid=(B, T // tq),
        in_specs=[pl.BlockSpec((1, tq, hq), lambda b, t: (b, t, 0)),
                  pl.BlockSpec((N_HEADS, tq, HEAD_DIM), lambda b, t: (0, 0, 0), pipeline_mode=once),
                  pl.BlockSpec((1, tq, IDX_HEADS * IDX_DIM), lambda b, t: (b, t, 0)),
                  pl.BlockSpec((1, tq, IDX_HEADS), lambda b, t: (b, t, 0)),
                  pl.BlockSpec((1, nc, N_KV, 2 * HEAD_DIM, ck), lambda b, t: (b, 0, 0, 0, 0), pipeline_mode=once),
                  pl.BlockSpec((1, Sp, N_KV * 2 * HEAD_DIM), lambda b, t: (b, 0, 0), pipeline_mode=once),
                  pl.BlockSpec((1, nc, IDX_DIM, ck), lambda b, t: (b, 0, 0, 0), pipeline_mode=once)],
        out_specs=pl.BlockSpec((1, tq, hq), lambda b, t: (b, t, 0)),
        out_shape=jax.ShapeDtypeStruct((B, T, hq), bf16),
        scratch_shapes=[pltpu.VMEM((n_grp * cpg, tq, ck), jnp.int32),
                        pltpu.VMEM((n_grp, PLANE_BITS, tq, LANES), jnp.int32),
                        pltpu.VMEM((n_grp, tq, LANES), jnp.int32),
                        pltpu.VMEM((N_KV, GQA_G * tq, 2 * HEAD_DIM), bf16),
                        pltpu.VMEM((N_KV, GQA_G * tq, ck), f32),
                        pltpu.VMEM((N_KV, GQA_G * tq, ck), f32),
                        pltpu.VMEM((N_KV, GQA_G * tq, 1), f32),
                        pltpu.VMEM((N_KV, GQA_G * tq, 2 * HEAD_DIM), f32)],
        compiler_params=_cparams(2),
    )(q, qs, qi, wi, kt, va, kit)


SPAN = PLANE_BITS * SUBLANES
ATTN_TQ = 256


def _sparse_attn_t_kernel(qt_ref, qst_ref, qit_ref, wit_ref, k_ref, pos_ref, vt_ref, ki_ref, ot_ref,
                          key_ref, plane_ref, act_ref, qa_ref, sa_ref, sb_ref, m_ref, acc_ref,
                          *, tq, ck, S, pos0, topk):
    spc = ck // SPAN
    wr = spc * SUBLANES
    ltiles = tq // LANES
    qb = pl.program_id(1)
    p_first = pos0 + qb * tq
    limit = jnp.minimum(S, ((p_first + tq - 1) // CHUNK + 1) * CHUNK)
    nkc = (limit + ck - 1) // ck
    qpos = p_first + lax.broadcasted_iota(jnp.int32, (1, tq), 1)
    qchunk = qpos // CHUNK
    row_pos = lax.broadcasted_iota(jnp.int32, (ck, tq), 0)
    wit = wit_ref[0]

    for hd in range(N_HEADS):
        n, g = divmod(hd, GQA_G)
        qa_ref[n, 0:HEAD_DIM, g * tq:(g + 1) * tq] = qt_ref[0, hd * HEAD_DIM:(hd + 1) * HEAD_DIM, :]
        qa_ref[n, HEAD_DIM:2 * HEAD_DIM, g * tq:(g + 1) * tq] = qst_ref[hd]

    def score_body(c, carry):
        kic = ki_ref[0, pl.ds(pl.multiple_of(c * ck, ck), ck), :]
        sc = jnp.zeros((ck, tq), f32)
        for hd in range(IDX_HEADS):
            d = _mm(kic, qit_ref[0, hd * IDX_DIM:(hd + 1) * IDX_DIM, :])
            sc = sc + wit[hd:hd + 1, :] * jnp.maximum(d, 0.0)
        kpos = c * ck + row_pos
        vis = (kpos // CHUNK <= qchunk) & (kpos < S)
        bits = pltpu.bitcast(sc, jnp.int32)
        key = jnp.where(vis, bits ^ ((bits >> 31) & 0x7FFFFFFF), INT_MIN)
        key_ref[c] = key
        for u in range(spc):
            for lt in range(ltiles):
                w = [key[u * SPAN + k * SUBLANES:u * SPAN + (k + 1) * SUBLANES, lt * LANES:(lt + 1) * LANES]
                     for k in range(PLANE_BITS)]
                for j, msk in BUTTERFLY:
                    for k in range(PLANE_BITS):
                        if k & j == 0:
                            tmp = (w[k] ^ lax.shift_right_logical(w[k + j], jnp.int32(j))) & msk
                            w[k] = w[k] ^ tmp
                            w[k + j] = w[k + j] ^ (tmp << j)
                rows = pl.ds(pl.multiple_of(c * wr + u * SUBLANES, SUBLANES), SUBLANES)
                lanes = slice(lt * LANES, (lt + 1) * LANES)
                plane_ref[0, rows, lanes] = ~w[0]
                for i in range(1, PLANE_BITS):
                    plane_ref[i, rows, lanes] = w[i]
        act_ref[pl.ds(pl.multiple_of(c * wr, wr), wr), :] = jnp.full((wr, tq), -1, jnp.int32)
        return carry

    lax.fori_loop(0, nkc, score_body, 0)

    def chunk_rows(c):
        return pl.ds(pl.multiple_of(c * wr, wr), wr)

    def pair_body(i, carry):
        need, thr_u = carry

        def csum(c, parts):
            rows = chunk_rows(c)
            a, p1, p2 = act_ref[rows, :], plane_ref[2 * i, rows, :], plane_ref[2 * i + 1, rows, :]
            a1 = a & p1
            a11 = a1 & p2
            return (parts[0] + lax.population_count(a11), parts[1] + lax.population_count(a1 ^ a11),
                    parts[2] + lax.population_count((a ^ a1) & p2))

        zero = jnp.zeros((wr, tq), jnp.int32)
        parts = lax.fori_loop(0, nkc, csum, (zero, zero, zero))
        t1 = jnp.sum(parts[0], axis=0, keepdims=True)
        t2 = t1 + jnp.sum(parts[1], axis=0, keepdims=True)
        t3 = t2 + jnp.sum(parts[2], axis=0, keepdims=True)
        hi = need <= t2
        lo = (need <= t1) | (jnp.logical_not(hi) & (need <= t3))
        flip1 = jnp.where(hi, 0, -1)
        flip2 = jnp.where(lo, 0, -1)

        def cupd(c, c_):
            rows = chunk_rows(c)
            act_ref[rows, :] = (act_ref[rows, :] & (plane_ref[2 * i, rows, :] ^ flip1)
                                & (plane_ref[2 * i + 1, rows, :] ^ flip2))
            return c_

        lax.fori_loop(0, nkc, cupd, 0)
        need = need - jnp.where(hi, jnp.where(lo, 0, t1), jnp.where(lo, t2, t3))
        bits = jnp.where(hi, 2, 0) | jnp.where(lo, 1, 0)
        return need, thr_u | lax.shift_left(bits, 30 - 2 * i)

    need, thr_u = lax.fori_loop(0, PLANE_BITS // 2, pair_body,
                                (jnp.full((1, tq), topk, jnp.int32), jnp.zeros((1, tq), jnp.int32)))
    thr = thr_u ^ INT_MIN
    n_eq = jnp.sum(lax.fori_loop(0, nkc, lambda c, part: part + lax.population_count(act_ref[chunk_rows(c), :]),
                                 jnp.zeros((wr, tq), jnp.int32)), axis=0, keepdims=True)

    def count(pred):
        def body(c, part):
            return part + jnp.sum(pred(key_ref[c], c).astype(jnp.int32), axis=0, keepdims=True)
        return lax.fori_loop(0, nkc, body, jnp.zeros((1, tq), jnp.int32))

    real = thr != INT_MIN
    excess = real & (n_eq > need)
    pos_bits = max(1, (ck * key_ref.shape[0] - 1).bit_length())

    def tie_search():
        def pbody(i, ans):
            cand = ans | lax.shift_left(jnp.int32(1), pos_bits - 1 - i)
            cnt = count(lambda kc, c: (kc == thr) & (c * ck + row_pos < cand))
            return jnp.where(cnt < need, cand, ans)
        return lax.fori_loop(0, pos_bits, pbody, jnp.zeros((1, tq), jnp.int32))

    any_excess = jnp.max(excess.astype(jnp.int32)) > 0
    tie_lim = lax.cond(any_excess, tie_search, lambda: jnp.zeros((1, tq), jnp.int32))
    big = jnp.int32(ck * key_ref.shape[0])
    tie_lim = jnp.where(excess, tie_lim, jnp.where(real, big, -1))

    def qk_chunk(c, s_ref):
        rows = pl.ds(pl.multiple_of(c * ck, ck), ck)
        for n in range(N_KV):
            ka = jnp.concatenate([k_ref[0, rows, n * HEAD_DIM:(n + 1) * HEAD_DIM], pos_ref[rows, :]], axis=1)
            s_ref[n] = _mm(ka, qa_ref[n])

    def attn_chunk(c, s_ref, first):
        kc = key_ref[c]
        kpos = c * ck + row_pos
        sel = (kc > thr) | ((kc == thr) & (kpos <= tie_lim))
        bias = jnp.where(sel, 0.0, NEG)
        if first:
            ahead = jnp.maximum(kpos - qpos, 0).astype(f32)
        for n in range(N_KV):
            ps, alphas = [], []
            for g in range(GQA_G):
                cols = slice(g * tq, (g + 1) * tq)
                sg = s_ref[n, :, cols] + bias
                if first:
                    sg = sg - (2.0 * _slope(n * GQA_G + g)) * ahead
                    m_new = jnp.max(sg, axis=0, keepdims=True)
                else:
                    m_prev = m_ref[n, :, cols]
                    m_new = jnp.maximum(m_prev, jnp.max(sg, axis=0, keepdims=True))
                    alphas.append(jnp.exp(m_prev - m_new))
                m_ref[n, :, cols] = m_new
                ps.append(jnp.exp(sg - m_new).astype(bf16))
            vat = jnp.concatenate([vt_ref[0, c, n], jnp.ones((HEAD_DIM, ck), bf16)], axis=0)
            pv = _mm(vat, jnp.concatenate(ps, axis=1))
            acc_ref[n] = pv if first else jnp.concatenate(alphas, axis=1) * acc_ref[n] + pv

    n_rest = nkc - 1
    qk_chunk(n_rest, sa_ref)
    qk_chunk(0, sb_ref)
    attn_chunk(n_rest, sa_ref, True)

    def attn_body(i, carry):
        qk_chunk(2 * i + 1, sa_ref)
        attn_chunk(2 * i, sb_ref, False)
        qk_chunk(2 * i + 2, sb_ref)
        attn_chunk(2 * i + 1, sa_ref, False)
        return carry

    lax.fori_loop(0, n_rest // 2, attn_body, 0)

    @pl.when(n_rest % 2 == 1)
    def _():
        attn_chunk(n_rest - 1, sb_ref, False)

    for hd in range(N_HEADS):
        n, g = divmod(hd, GQA_G)
        a = acc_ref[n, :, g * tq:(g + 1) * tq]
        ot_ref[0, hd * HEAD_DIM:(hd + 1) * HEAD_DIM, :] = (a[:HEAD_DIM] / a[HEAD_DIM:]).astype(bf16)


def _sparse_attn_t(q, qi, wi, k_all, v_all, ki_all, pos0, tq, ck, topk):
    B, T, _ = q.shape
    S = k_all.shape[1]
    nc = -(-S // ck)
    Sp = nc * ck
    assert ck % SPAN == 0 and tq % LANES == 0 and T % tq == 0
    assert Sp <= 256 * LANES
    for qb in range(T // tq):
        p_first = pos0 + qb * tq
        limit = min(S, ((p_first + tq - 1) // CHUNK + 1) * CHUNK)
        assert p_first >= (-(-limit // ck) - 1) * ck
    pad = lambda a: jnp.pad(a.astype(bf16), ((0, 0), (0, Sp - S), (0, 0)))
    j = jnp.arange(Sp)
    pos_cols = jnp.zeros((Sp, HEAD_DIM), f32).at[:, 0].set(j // LANES).at[:, 1].set(j % LANES).astype(bf16)
    vt = pad(v_all).reshape(B, Sp, N_KV, HEAD_DIM).transpose(0, 2, 3, 1)
    vt = vt.reshape(B, N_KV, HEAD_DIM, nc, ck).transpose(0, 3, 1, 2, 4)
    assert all((8.0 * (hd + 1) / N_HEADS).is_integer() for hd in range(N_HEADS))
    slopes = jnp.asarray([_slope(hd) for hd in range(N_HEADS)], f32)
    qst = jnp.zeros((N_HEADS, HEAD_DIM, tq), f32).at[:, 0, :].set(LANES * slopes[:, None])
    qst = qst.at[:, 1, :].set(slopes[:, None]).astype(bf16)

    kern = functools.partial(_sparse_attn_t_kernel, tq=tq, ck=ck, S=S, pos0=pos0, topk=topk)
    hq = N_HEADS * HEAD_DIM
    once = pl.Buffered(1)
    wr = ck // SPAN * SUBLANES
    ot = pl.pallas_call(
        kern,
        grid=(B, T // tq),
        in_specs=[pl.BlockSpec((1, hq, tq), lambda b, t: (b, 0, t)),
                  pl.BlockSpec((N_HEADS, HEAD_DIM, tq), lambda b, t: (0, 0, 0), pipeline_mode=once),
                  pl.BlockSpec((1, IDX_HEADS * IDX_DIM, tq), lambda b, t: (b, 0, t)),
                  pl.BlockSpec((1, IDX_HEADS, tq), lambda b, t: (b, 0, t)),
                  pl.BlockSpec((1, Sp, N_KV * HEAD_DIM), lambda b, t: (b, 0, 0), pipeline_mode=once),
                  pl.BlockSpec((Sp, HEAD_DIM), lambda b, t: (0, 0), pipeline_mode=once),
                  pl.BlockSpec((1, nc, N_KV, HEAD_DIM, ck), lambda b, t: (b, 0, 0, 0, 0), pipeline_mode=once),
                  pl.BlockSpec((1, Sp, IDX_DIM), lambda b, t: (b, 0, 0), pipeline_mode=once)],
        out_specs=pl.BlockSpec((1, hq, tq), lambda b, t: (b, 0, t)),
        out_shape=jax.ShapeDtypeStruct((B, hq, T), bf16),
        scratch_shapes=[pltpu.VMEM((nc, ck, tq), jnp.int32),
                        pltpu.VMEM((PLANE_BITS, nc * wr, tq), jnp.int32),
                        pltpu.VMEM((nc * wr, tq), jnp.int32),
                        pltpu.VMEM((N_KV, 2 * HEAD_DIM, GQA_G * tq), bf16),
                        pltpu.VMEM((N_KV, ck, GQA_G * tq), f32),
                        pltpu.VMEM((N_KV, ck, GQA_G * tq), f32),
                        pltpu.VMEM((N_KV, 1, GQA_G * tq), f32),
                        pltpu.VMEM((N_KV, 2 * HEAD_DIM, GQA_G * tq), f32)],
        compiler_params=_cparams(2),
    )(q.transpose(0, 2, 1), qst, qi.transpose(0, 2, 1), wi.transpose(0, 2, 1), pad(k_all), pos_cols, vt,
      pad(ki_all))
    return ot.transpose(0, 2, 1)


def _merge_kernel(x_ref, h_ref, ya_ref, yb_ref, o_ref, g1_ref, wgl_ref, wpc_ref, wo_ref,
                  gn2_ref, sc2_ref, sh2_ref, x1_ref, h2_ref, *, bb, tt):
    D = D_MODEL
    M = bb * tt
    h = h_ref[...].reshape(M, D)
    yc = _mm(o_ref[...].reshape(M, D), wpc_ref[...])
    m = _sigmoid(_mm(h, wgl_ref[:, 0:D])) * ya_ref[...].reshape(M, D).astype(f32)
    m = m + _sigmoid(_mm(h, wgl_ref[:, D:2 * D])) * yb_ref[...].reshape(M, D).astype(f32)
    m = m + _sigmoid(_mm(h, wgl_ref[:, 2 * D:3 * D])) * yc
    x1 = x_ref[...] + g1_ref[...] * _mm(m.astype(bf16), wo_ref[...]).reshape(bb, tt, D)
    x1_ref[...] = x1
    h2 = _rms(x1, gn2_ref[...]) * (1.0 + sc2_ref[...]) + sh2_ref[...]
    h2_ref[...] = h2.astype(bf16)


def _merge(x, h, ya, yb, o, g1, wgl, wpc, wo, gn2, sc2, sh2):
    B, T, D = x.shape
    bb, tt = _tiles(B, T)
    kern = functools.partial(_merge_kernel, bb=bb, tt=tt)
    row = _row_spec(bb, tt, D)
    vec = _batch_spec(bb, 1, D)
    return pl.pallas_call(
        kern,
        grid=(B // bb, T // tt),
        in_specs=[row, row, row, row, row, vec, _const_spec((D, 3 * D)), _const_spec((D, D)),
                  _const_spec((D, D)), _const_spec((1, 1, D)), vec, vec],
        out_specs=[row, row],
        out_shape=[jax.ShapeDtypeStruct((B, T, D), f32), jax.ShapeDtypeStruct((B, T, D), bf16)],
        compiler_params=_cparams(2),
    )(x, h, ya, yb, o, g1, wgl, wpc, wo, gn2.reshape(1, 1, D), sc2, sh2)


FF_TILE = 1024


def _mlp_kernel(x1_ref, h2_ref, g2_ref, w1_ref, b1_ref, w2_ref, b2_ref, x2_ref, *, bb, tt):
    D = D_MODEL
    M = bb * tt
    h2 = h2_ref[...].reshape(M, D)
    acc = jnp.zeros((M, D), f32)
    for c0 in range(0, D_FF, FF_TILE):
        f = jnp.maximum(_mm(h2, w1_ref[:, c0:c0 + FF_TILE]) + b1_ref[:, c0:c0 + FF_TILE], 0.0)
        acc = acc + _mm((f * f).astype(bf16), w2_ref[c0:c0 + FF_TILE, :])
    f = (acc + b2_ref[...]).reshape(bb, tt, D)
    x2_ref[...] = x1_ref[...] + g2_ref[...] * f


def _mlp(x1, h2, g2, w1, b1, w2, b2):
    B, T, D = x1.shape
    bb, tt = _tiles(B, T)
    kern = functools.partial(_mlp_kernel, bb=bb, tt=tt)
    row = _row_spec(bb, tt, D)
    return pl.pallas_call(
        kern,
        grid=(B // bb, T // tt),
        in_specs=[row, row, _batch_spec(bb, 1, D), _const_spec((D, D_FF)), _const_spec((1, D_FF)),
                  _const_spec((D_FF, D)), _const_spec((1, D))],
        out_specs=row,
        out_shape=jax.ShapeDtypeStruct((B, T, D), f32),
        compiler_params=_cparams(2),
    )(x1, h2, g2, w1, b1.reshape(1, D_FF), w2, b2.reshape(1, D))


def _split_w_in(w_in):
    D = D_MODEL
    w = w_in.astype(bf16)
    o_b = 2 * D
    o_q = 4 * D
    o_gl = o_q + N_HEADS * HEAD_DIM + 2 * N_KV * HEAD_DIM + IDX_HEADS * IDX_DIM + IDX_DIM + IDX_HEADS
    n_c = o_gl - o_q
    w_c = jnp.pad(w[:, o_q:o_gl], ((0, 0), (0, QKV_COLS - n_c)))
    return dict(wa1=w[:, 0:D], wa2=w[:, D:2 * D], wbu=w[:, o_b:o_b + D], wbv=w[:, o_b + D:o_b + 2 * D],
                w_c=w_c, wgl=w[:, o_gl:o_gl + 3 * D])


def _layer(x, mod, cache, W, pos0):
    B, T, D = x.shape
    sh1, sc1, g1, sh2, sc2, g2 = mod
    win = W['w_in']
    h = _prenorm(x, W['g_norm1'], sc1, sh1)

    hist = jnp.zeros((B, HIST, D), f32) if cache is None else cache[3]
    ya, new_conv = _branch_conv(h, hist, win['wa1'], win['wa2'], W['w_dw'], W['b_dw'],
                                W['ln_c_g'], W['ln_c_b'], W['w_pa'])

    L = GMLP_CHUNK if cache is None else T
    res = _branch_gmlp(h, win['wbu'], win['wbv'], W['ln_v_g'], W['ln_v_b'], W['w_s'], W['b_s'],
                       W['w_pb'], L, cache is not None)
    yb = res[0]
    vg = res[1] if cache is not None else None

    q, k, v, qi, ki, wi = _attn_proj(h, win['w_c'], W['g_q'], W['g_k'])
    if cache is None:
        k_all, v_all, ki_all = k, v, ki
        ck = 512
    else:
        k_all = jnp.concatenate([cache[0].reshape(B, -1, N_KV * HEAD_DIM), k], axis=1)
        v_all = jnp.concatenate([cache[1].reshape(B, -1, N_KV * HEAD_DIM), v], axis=1)
        ki_all = jnp.concatenate([cache[2], ki], axis=1)
        ck = 256
    topk = min(TOPK_MAX, k_all.shape[1] // 4)
    if T % ATTN_TQ == 0:
        o = _sparse_attn_t(q, qi, wi, k_all, v_all, ki_all, pos0, ATTN_TQ, ck, topk)
    else:
        o = _sparse_attn(q, qi, wi, k_all, v_all, ki_all, pos0, min(Q_BLOCK, T), ck, topk)

    x1, h2 = _merge(x, h, ya, yb, o, g1, win['wgl'], W['w_pc'], W['w_o'], W['g_norm2'], sc2, sh2)
    x2 = _mlp(x1, h2, g2, W['w_1'], W['b_1'], W['w_2'], W['b_2'])
    return x2, (k.reshape(B, T, N_KV, HEAD_DIM), v.reshape(B, T, N_KV, HEAD_DIM), ki, new_conv, vg)


def kernel(x_prompt, x_sample, c_prompt, c_sample, cache_k, cache_v, cache_kidx, cache_conv, w_ada, b_ada, g_norm1, w_in, w_dw, b_dw, ln_c_g, ln_c_b, w_pa, ln_v_g, ln_v_b, w_s, b_s, w_pb, g_q, g_k, w_pc, w_o, g_norm2, w_1, b_1, w_2, b_2):
    depth = w_in.shape[0]
    D = D_MODEL
    Bp, Bs = x_prompt.shape[0], x_sample.shape[0]
    past = cache_k.shape[2]
    c_all = jnp.concatenate([c_prompt, c_sample], axis=0)
    n_c = c_all.shape[0]
    n_pad = -(-n_c // SUBLANES) * SUBLANES
    mod_all = _ada(jnp.pad(c_all, ((0, n_pad - n_c), (0, 0))), w_ada.astype(bf16), b_ada)

    def mods(l, lo, n):
        m = mod_all[l, lo:lo + n]
        return [m[:, None, i * D:(i + 1) * D] for i in range(6)]

    hp, hs = x_prompt, x_sample
    outs_p, outs_s = [], []
    for l in range(depth):
        W = dict(g_norm1=g_norm1[l], w_in=_split_w_in(w_in[l]), w_dw=w_dw[l], b_dw=b_dw[l],
                 ln_c_g=ln_c_g[l], ln_c_b=ln_c_b[l], w_pa=w_pa[l].astype(bf16), ln_v_g=ln_v_g[l],
                 ln_v_b=ln_v_b[l], w_s=w_s[l], b_s=b_s[l], w_pb=w_pb[l].astype(bf16), g_q=g_q[l],
                 g_k=g_k[l], w_pc=w_pc[l].astype(bf16), w_o=w_o[l].astype(bf16), g_norm2=g_norm2[l],
                 w_1=w_1[l].astype(bf16), b_1=b_1[l], w_2=w_2[l].astype(bf16), b_2=b_2[l])
        hp, op = _layer(hp, mods(l, 0, Bp), None, W, 0)
        hs, os_ = _layer(hs, mods(l, Bp, Bs), (cache_k[l], cache_v[l], cache_kidx[l], cache_conv[l]), W, past)
        outs_p.append(op)
        outs_s.append(os_)
    stack = lambda outs, i: jnp.stack([o[i] for o in outs])
    return (hp, hs,
            stack(outs_p, 0), stack(outs_p, 1), stack(outs_p, 2), stack(outs_p, 3),
            stack(outs_s, 0), stack(outs_s, 1), stack(outs_s, 2), stack(outs_s, 3), stack(outs_s, 4))
```

```python
import functools

import jax
import jax.numpy as jnp
from jax import lax
from jax.experimental import pallas as pl
from jax.experimental.pallas import tpu as pltpu

D_MODEL = 1024
CHUNK = 64
CONV_W = 31
HIST = CONV_W - 1
GMLP_CHUNK = 128
GMLP_GROUPS = 8
GMLP_GW = D_MODEL // GMLP_GROUPS
N_HEADS = 8
HEAD_DIM = 128
N_KV = 2
GQA_G = N_HEADS // N_KV
IDX_HEADS = 8
IDX_DIM = 64
TOPK_MAX = 256
Q_BLOCK = 128
D_FF = 4 * D_MODEL
EPS = 1e-6

LANES = 128
SUBLANES = 8
HALO = 32
ROW_TILE = 512
VMEM_LIMIT = 56 * 1024 * 1024
INT_MIN = -2 ** 31
NEG = -1e30

f32 = jnp.float32
bf16 = jnp.bfloat16


def _cparams(n_axes):
    return pltpu.CompilerParams(dimension_semantics=("arbitrary",) * n_axes,
                                vmem_limit_bytes=VMEM_LIMIT)


def _tiles(B, T):
    if T >= ROW_TILE:
        assert T % ROW_TILE == 0
        return 1, ROW_TILE
    assert ROW_TILE % T == 0 and B % (ROW_TILE // T) == 0
    return ROW_TILE // T, T


def _row_spec(bb, tt, n):
    return pl.BlockSpec((bb, tt, n), lambda b, t: (b, t, 0))


def _batch_spec(bb, r, n):
    return pl.BlockSpec((bb, r, n), lambda b, t: (b, 0, 0))


def _const_spec(shape):
    nd = len(shape)
    return pl.BlockSpec(shape, lambda b, t: (0,) * nd, pipeline_mode=pl.Buffered(1))


def _mm(a, w):
    return jnp.dot(a, w, preferred_element_type=f32)


def _sigmoid(x):
    return 1.0 / (1.0 + jnp.exp(-x))


def _silu(x):
    return x * _sigmoid(x)


def _gelu_tanh(x):
    return 0.5 * x * (1.0 + jnp.tanh(0.7978845608028654 * (x + 0.044715 * (x * x * x))))


def _rms(x, g):
    return x * lax.rsqrt(jnp.mean(x * x, axis=-1, keepdims=True) + EPS) * g


def _ln(x, g, b):
    mu = jnp.mean(x, axis=-1, keepdims=True)
    xc = x - mu
    var = jnp.mean(xc * xc, axis=-1, keepdims=True)
    return xc * lax.rsqrt(var + EPS) * g + b


def _ada_kernel(c_ref, w_ref, b_ref, o_ref):
    c = c_ref[...]
    o_ref[0] = _mm(_silu(c).astype(bf16), w_ref[0]) + b_ref[0]


def _ada(c_all, w_ada, b_ada):
    depth, d, n = w_ada.shape
    bp = c_all.shape[0]
    tn = 1024
    return pl.pallas_call(
        _ada_kernel,
        grid=(depth, n // tn),
        in_specs=[pl.BlockSpec((bp, d), lambda l, j: (0, 0)),
                  pl.BlockSpec((1, d, tn), lambda l, j: (l, 0, j)),
                  pl.BlockSpec((1, 1, tn), lambda l, j: (l, 0, j))],
        out_specs=pl.BlockSpec((1, bp, tn), lambda l, j: (l, 0, j)),
        out_shape=jax.ShapeDtypeStruct((depth, bp, n), f32),
        compiler_params=_cparams(2),
    )(c_all, w_ada, b_ada.reshape(depth, 1, n))


def _prenorm_kernel(x_ref, g_ref, sc_ref, sh_ref, h_ref):
    h = _rms(x_ref[...], g_ref[...]) * (1.0 + sc_ref[...]) + sh_ref[...]
    h_ref[...] = h.astype(bf16)


def _prenorm(x, g, sc, sh):
    B, T, D = x.shape
    bb, tt = _tiles(B, T)
    return pl.pallas_call(
        _prenorm_kernel,
        grid=(B // bb, T // tt),
        in_specs=[_row_spec(bb, tt, D), _const_spec((1, 1, D)),
                  _batch_spec(bb, 1, D), _batch_spec(bb, 1, D)],
        out_specs=_row_spec(bb, tt, D),
        out_shape=jax.ShapeDtypeStruct((B, T, D), bf16),
        compiler_params=_cparams(2),
    )(x, g.reshape(1, 1, D), sc, sh)


CONV_ROWS = 64
CONV_COLS = 256


def _conv_kernel(h_ref, hist_ref, wa1_ref, wa2_ref, wdw_ref, bdw_ref, lng_ref, lnb_ref, wpa_ref,
                 ya_ref, newc_ref, apad_ref, conv_ref, shift_ref, *, bb, tt):
    D = D_MODEL
    t = pl.program_id(1)

    @pl.when(t == 0)
    def _():
        apad_ref[:, 0:HALO - HIST, :] = jnp.zeros((bb, HALO - HIST, D), f32)
        apad_ref[:, HALO - HIST:HALO, :] = hist_ref[...]

    h = h_ref[...].reshape(bb * tt, D)
    a = _mm(h, wa1_ref[...]) * _sigmoid(_mm(h, wa2_ref[...]))
    apad_ref[:, HALO:HALO + tt, :] = a.reshape(bb, tt, D)

    base = HALO - HIST
    n_sh = HALO + tt - SUBLANES
    for b in range(bb):
        for c0 in range(0, D, CONV_COLS):
            cols = slice(c0, c0 + CONV_COLS)
            for s in range(1, SUBLANES):
                shift_ref[s - 1, 0:n_sh, :] = apad_ref[b, s:s + n_sh, cols]
            for r0 in range(0, tt, CONV_ROWS):
                acc = jnp.zeros((CONV_ROWS, CONV_COLS), f32)
                for j in range(CONV_W):
                    q8, s = divmod(base + j, SUBLANES)
                    lo = r0 + SUBLANES * q8
                    src = (apad_ref[b, lo:lo + CONV_ROWS, cols] if s == 0
                           else shift_ref[s - 1, lo:lo + CONV_ROWS, :])
                    acc = acc + wdw_ref[j:j + 1, cols] * src
                conv_ref[b * tt + r0:b * tt + r0 + CONV_ROWS, cols] = acc

    tail = apad_ref[:, HALO + tt - HIST:HALO + tt, :]
    newc_ref[...] = tail
    apad_ref[:, HALO - HIST:HALO, :] = tail

    y = conv_ref[...] + bdw_ref[...]
    z = _silu(_ln(y, lng_ref[...], lnb_ref[...]))
    ya_ref[...] = _mm(z.astype(bf16), wpa_ref[...]).astype(bf16).reshape(bb, tt, D)


def _branch_conv(h, hist, wa1, wa2, wdw, bdw, lng, lnb, wpa):
    B, T, D = h.shape
    bb, tt = _tiles(B, T)
    kern = functools.partial(_conv_kernel, bb=bb, tt=tt)
    return pl.pallas_call(
        kern,
        grid=(B // bb, T // tt),
        in_specs=[_row_spec(bb, tt, D), _batch_spec(bb, HIST, D),
                  _const_spec((D, D)), _const_spec((D, D)), _const_spec((CONV_W, D)),
                  _const_spec((1, D)), _const_spec((1, D)), _const_spec((1, D)), _const_spec((D, D))],
        out_specs=[_row_spec(bb, tt, D), _batch_spec(bb, HIST, D)],
        out_shape=[jax.ShapeDtypeStruct((B, T, D), bf16), jax.ShapeDtypeStruct((B, HIST, D), f32)],
        scratch_shapes=[pltpu.VMEM((bb, HALO + tt, D), f32), pltpu.VMEM((bb * tt, D), f32),
                        pltpu.VMEM((SUBLANES - 1, HALO + tt, CONV_COLS), f32)],
        compiler_params=_cparams(2),
    )(h, hist, wa1, wa2, wdw, bdw.reshape(1, D), lng.reshape(1, D), lnb.reshape(1, D), wpa)


def _gmlp_kernel(h_ref, wbu_ref, wbv_ref, lng_ref, lnb_ref, wm_ref, bs_ref, wpb_ref,
                 yb_ref, *rest, bb, tt, L, emit_v):
    D = D_MODEL
    if emit_v:
        vg_ref, s_ref = rest
    else:
        (s_ref,) = rest
    M = bb * tt
    h = h_ref[...].reshape(M, D)
    u = _gelu_tanh(_mm(h, wbu_ref[...]))
    vg = _ln(_gelu_tanh(_mm(h, wbv_ref[...])), lng_ref[...], lnb_ref[...])
    if emit_v:
        vg_ref[...] = vg.reshape(bb, tt, D)
    vgb = vg.astype(bf16)
    for c in range(M // L):
        for g in range(GMLP_GROUPS):
            blk = vgb[c * L:(c + 1) * L, g * GMLP_GW:(g + 1) * GMLP_GW]
            s_ref[c * L:(c + 1) * L, g * GMLP_GW:(g + 1) * GMLP_GW] = _mm(wm_ref[g], blk) + bs_ref[:, g * GMLP_GW:(g + 1) * GMLP_GW]
    yb_ref[...] = _mm((u * s_ref[...]).astype(bf16), wpb_ref[...]).astype(bf16).reshape(bb, tt, D)


def _branch_gmlp(h, wbu, wbv, lng, lnb, w_s, b_s, wpb, L, emit_v):
    B, T, D = h.shape
    bb, tt = _tiles(B, T)
    assert tt % L == 0
    tril = jnp.tril(jnp.ones((L, L), dtype=bool))
    wm = jnp.where(tril[None], w_s[:, :L, :L], 0).astype(bf16)
    bs = jnp.repeat(jnp.swapaxes(b_s[:, :L], 0, 1), GMLP_GW, axis=1)
    kern = functools.partial(_gmlp_kernel, bb=bb, tt=tt, L=L, emit_v=emit_v)
    out_specs = [_row_spec(bb, tt, D)]
    out_shape = [jax.ShapeDtypeStruct((B, T, D), bf16)]
    if emit_v:
        out_specs.append(_row_spec(bb, tt, D))
        out_shape.append(jax.ShapeDtypeStruct((B, T, D), f32))
    return pl.pallas_call(
        kern,
        grid=(B // bb, T // tt),
        in_specs=[_row_spec(bb, tt, D), _const_spec((D, D)), _const_spec((D, D)),
                  _const_spec((1, D)), _const_spec((1, D)), _const_spec((GMLP_GROUPS, L, L)),
                  _const_spec((L, D)), _const_spec((D, D))],
        out_specs=out_specs,
        out_shape=out_shape,
        scratch_shapes=[pltpu.VMEM((bb * tt, D), f32)],
        compiler_params=_cparams(2),
    )(h, wbu, wbv, lng.reshape(1, D), lnb.reshape(1, D), wm, bs, wpb)


QKV_COLS = N_HEADS * HEAD_DIM + 2 * N_KV * HEAD_DIM + IDX_HEADS * IDX_DIM + LANES


def _attn_proj_kernel(h_ref, w_ref, gq_ref, gk_ref, q_ref, k_ref, v_ref, qi_ref, ki_ref, wi_ref,
                      *, bb, tt, queries_on_lanes):
    D = D_MODEL
    M = bb * tt
    p = _mm(h_ref[...].reshape(M, D), w_ref[...])
    o = 0
    for hd in range(N_HEADS):
        qh = _rms(p[:, o:o + HEAD_DIM], gq_ref[...]) * HEAD_DIM ** -0.5
        if queries_on_lanes:
            q_ref[0, hd * HEAD_DIM:(hd + 1) * HEAD_DIM, :] = qh.T.astype(bf16)
        else:
            q_ref[:, :, hd * HEAD_DIM:(hd + 1) * HEAD_DIM] = qh.astype(bf16).reshape(bb, tt, HEAD_DIM)
        o += HEAD_DIM
    for n in range(N_KV):
        k_ref[:, :, n * HEAD_DIM:(n + 1) * HEAD_DIM] = _rms(
            p[:, o:o + HEAD_DIM], gk_ref[...]).reshape(bb, tt, HEAD_DIM)
        o += HEAD_DIM
    v_ref[...] = p[:, o:o + N_KV * HEAD_DIM].reshape(bb, tt, N_KV * HEAD_DIM)
    o += N_KV * HEAD_DIM
    if queries_on_lanes:
        for c0 in range(0, IDX_HEADS * IDX_DIM, LANES):
            qi_ref[0, c0:c0 + LANES, :] = p[:, o + c0:o + c0 + LANES].T.astype(bf16)
    else:
        qi_ref[...] = p[:, o:o + IDX_HEADS * IDX_DIM].astype(bf16).reshape(bb, tt, IDX_HEADS * IDX_DIM)
    o += IDX_HEADS * IDX_DIM
    ki_ref[...] = p[:, o:o + IDX_DIM].reshape(bb, tt, IDX_DIM)
    wi_scale = IDX_HEADS ** -0.5 * IDX_DIM ** -0.5
    if queries_on_lanes:
        tail = p[:, o:o + LANES].T
        wi_ref[0] = tail[IDX_DIM:IDX_DIM + IDX_HEADS, :] * wi_scale
    else:
        wi_ref[...] = (p[:, o + IDX_DIM:o + IDX_DIM + IDX_HEADS] * wi_scale).reshape(bb, tt, IDX_HEADS)


def _attn_proj(h, w_c, g_q, g_k, queries_on_lanes):
    B, T, D = h.shape
    bb, tt = _tiles(B, T)
    assert bb == 1 or not queries_on_lanes
    kern = functools.partial(_attn_proj_kernel, bb=bb, tt=tt, queries_on_lanes=queries_on_lanes)
    widths = (N_HEADS * HEAD_DIM, N_KV * HEAD_DIM, N_KV * HEAD_DIM, IDX_HEADS * IDX_DIM, IDX_DIM, IDX_HEADS)
    dtypes = (bf16, f32, f32, bf16, f32, f32)
    flipped = (queries_on_lanes, False, False, queries_on_lanes, False, queries_on_lanes)
    col_spec = lambda n: pl.BlockSpec((1, n, tt), lambda b, t: (b, 0, t))
    return pl.pallas_call(
        kern,
        grid=(B // bb, T // tt),
        in_specs=[_row_spec(bb, tt, D), _const_spec((D, QKV_COLS)),
                  _const_spec((1, HEAD_DIM)), _const_spec((1, HEAD_DIM))],
        out_specs=[col_spec(n) if fl else _row_spec(bb, tt, n) for n, fl in zip(widths, flipped)],
        out_shape=[jax.ShapeDtypeStruct((B, n, T) if fl else (B, T, n), dt)
                   for n, dt, fl in zip(widths, dtypes, flipped)],
        compiler_params=_cparams(2),
    )(h, w_c, g_q.reshape(1, HEAD_DIM), g_k.reshape(1, HEAD_DIM))


PLANE_BITS = 32
BUTTERFLY = ((16, 0x0000FFFF), (8, 0x00FF00FF), (4, 0x0F0F0F0F), (2, 0x33333333), (1, 0x55555555))


def _slope(hd):
    return 2.0 ** (-8.0 * (hd + 1) / N_HEADS)


def _sparse_attn_kernel(q_ref, qs_ref, qi_ref, wi_ref, kt_ref, va_ref, kit_ref, o_ref,
                        key_ref, plane_ref, act_ref, qa_ref, sa_ref, sb_ref, m_ref, acc_ref,
                        *, tq, ck, S, pos0, topk):
    tpc = ck // LANES
    cpg = PLANE_BITS // tpc
    rgrp = tq // SUBLANES
    qb = pl.program_id(1)
    p_first = pos0 + qb * tq
    limit = jnp.minimum(S, ((p_first + tq - 1) // CHUNK + 1) * CHUNK)
    nkc = (limit + ck - 1) // ck
    ngrp = (nkc + cpg - 1) // cpg
    qpos = p_first + lax.broadcasted_iota(jnp.int32, (tq, 1), 0)
    qchunk = qpos // CHUNK
    lane_pos = lax.broadcasted_iota(jnp.int32, (tq, ck), 1)
    wi = wi_ref[0]
    qi = qi_ref[0]
    qih = [qi[:, hd * IDX_DIM:(hd + 1) * IDX_DIM] for hd in range(IDX_HEADS)]

    for hd in range(N_HEADS):
        n, g = divmod(hd, GQA_G)
        qa_ref[n, g * tq:(g + 1) * tq, 0:HEAD_DIM] = q_ref[0, :, hd * HEAD_DIM:(hd + 1) * HEAD_DIM]
        qa_ref[n, g * tq:(g + 1) * tq, HEAD_DIM:2 * HEAD_DIM] = qs_ref[hd]

    def score_body(c, carry):
        kic = kit_ref[0, c]
        sc = jnp.zeros((tq, ck), f32)
        for hd in range(IDX_HEADS):
            sc = sc + wi[:, hd:hd + 1] * jnp.maximum(_mm(qih[hd], kic), 0.0)
        kpos = c * ck + lane_pos
        vis = (kpos // CHUNK <= qchunk) & (kpos < S)
        bits = pltpu.bitcast(sc, jnp.int32)
        key = bits ^ ((bits >> 31) & 0x7FFFFFFF)
        key_ref[c] = jnp.where(vis, key, INT_MIN)
        return carry

    lax.fori_loop(0, nkc, score_body, 0)

    def fill_body(c, carry):
        key_ref[c] = jnp.full((tq, ck), INT_MIN, jnp.int32)
        return carry

    lax.fori_loop(nkc, ngrp * cpg, fill_body, 0)

    def xpose_body(it, carry):
        g = it // rgrp
        rows = pl.ds(pl.multiple_of((it % rgrp) * SUBLANES, SUBLANES), SUBLANES)
        w = [key_ref[g * cpg + t // tpc, rows, (t % tpc) * LANES:(t % tpc + 1) * LANES]
             for t in range(PLANE_BITS)]
        for j, msk in BUTTERFLY:
            for k in range(PLANE_BITS):
                if k & j == 0:
                    tmp = (w[k] ^ lax.shift_right_logical(w[k + j], jnp.int32(j))) & msk
                    w[k] = w[k] ^ tmp
                    w[k + j] = w[k + j] ^ (tmp << j)
        plane_ref[g, 0, rows, :] = ~w[0]
        for i in range(1, PLANE_BITS):
            plane_ref[g, i, rows, :] = w[i]
        act_ref[g, rows, :] = jnp.full((SUBLANES, LANES), -1, jnp.int32)
        return carry

    lax.fori_loop(0, ngrp * rgrp, xpose_body, 0)

    def pair_body(i, carry):
        need, thr_u = carry

        def gsum(g, parts):
            a, p1, p2 = act_ref[g], plane_ref[g, 2 * i], plane_ref[g, 2 * i + 1]
            a1, a0 = a & p1, a & ~p1
            return (parts[0] + lax.population_count(a1 & p2), parts[1] + lax.population_count(a1 & ~p2),
                    parts[2] + lax.population_count(a0 & p2))

        zero = jnp.zeros((tq, LANES), jnp.int32)
        parts = lax.fori_loop(0, ngrp, gsum, (zero, zero, zero))
        cnt = jnp.sum(jnp.concatenate(parts, axis=0), axis=1, keepdims=True)
        t1 = cnt[0:tq]
        t2 = t1 + cnt[tq:2 * tq]
        t3 = t2 + cnt[2 * tq:3 * tq]
        hi = need <= t2
        lo = (need <= t1) | (jnp.logical_not(hi) & (need <= t3))
        flip1 = jnp.where(hi, 0, -1)
        flip2 = jnp.where(lo, 0, -1)

        def gupd(g, c_):
            act_ref[g] = act_ref[g] & (plane_ref[g, 2 * i] ^ flip1) & (plane_ref[g, 2 * i + 1] ^ flip2)
            return c_

        lax.fori_loop(0, ngrp, gupd, 0)
        need = need - jnp.where(hi, jnp.where(lo, 0, t1), jnp.where(lo, t2, t3))
        bits = jnp.where(hi, 2, 0) | jnp.where(lo, 1, 0)
        return need, thr_u | lax.shift_left(bits, 30 - 2 * i)

    need, thr_u = lax.fori_loop(0, PLANE_BITS // 2, pair_body,
                                (jnp.full((tq, 1), topk, jnp.int32), jnp.zeros((tq, 1), jnp.int32)))
    thr = thr_u ^ INT_MIN
    n_eq = jnp.sum(lax.fori_loop(0, ngrp, lambda g, part: part + lax.population_count(act_ref[g]),
                                 jnp.zeros((tq, LANES), jnp.int32)), axis=1, keepdims=True)

    def count(pred):
        def body(c, part):
            hit = pred(key_ref[c], c)
            for s in range(tpc):
                part = part + hit[:, s * LANES:(s + 1) * LANES].astype(jnp.int32)
            return part
        part = lax.fori_loop(0, nkc, body, jnp.zeros((tq, LANES), jnp.int32))
        return jnp.sum(part, axis=1, keepdims=True)

    real = thr != INT_MIN
    excess = real & (n_eq > need)
    pos_bits = max(1, (ck * key_ref.shape[0] - 1).bit_length())

    def tie_search():
        def pbody(i, ans):
            cand = ans | lax.shift_left(jnp.int32(1), pos_bits - 1 - i)
            cnt = count(lambda kc, c: (kc == thr) & (c * ck + lane_pos < cand))
            return jnp.where(cnt < need, cand, ans)
        return lax.fori_loop(0, pos_bits, pbody, jnp.zeros((tq, 1), jnp.int32))

    any_excess = jnp.max(excess.astype(jnp.int32)) > 0
    tie_lim = lax.cond(any_excess, tie_search, lambda: jnp.zeros((tq, 1), jnp.int32))
    big = jnp.int32(ck * key_ref.shape[0])
    tie_lim = jnp.where(excess, tie_lim, jnp.where(real, big, -1))

    m_ref[...] = jnp.full(m_ref.shape, NEG, f32)
    acc_ref[...] = jnp.zeros(acc_ref.shape, f32)

    def qk_chunk(c, s_ref):
        for n in range(N_KV):
            s_ref[n] = _mm(qa_ref[n], kt_ref[0, c, n])

    def attn_chunk(c, s_ref, last):
        start = pl.multiple_of(c * ck, ck)
        kc = key_ref[c]
        kpos = start + lane_pos
        sel = (kc > thr) | ((kc == thr) & (kpos <= tie_lim))
        bias = jnp.where(sel, 0.0, NEG)
        if last:
            ahead = jnp.maximum(kpos - qpos, 0).astype(f32)
        for n in range(N_KV):
            ps, alphas = [], []
            for g in range(GQA_G):
                rows = slice(g * tq, (g + 1) * tq)
                sg = s_ref[n, rows, :] + bias
                if last:
                    sg = sg - (2.0 * _slope(n * GQA_G + g)) * ahead
                m_prev = m_ref[n, rows]
                m_new = jnp.maximum(m_prev, jnp.max(sg, axis=1, keepdims=True))
                m_ref[n, rows] = m_new
                alphas.append(jnp.exp(m_prev - m_new))
                ps.append(jnp.exp(sg - m_new).astype(bf16))
            pv = _mm(jnp.concatenate(ps, axis=0),
                     va_ref[0, pl.ds(start, ck), n * 2 * HEAD_DIM:(n + 1) * 2 * HEAD_DIM])
            acc_ref[n] = jnp.concatenate(alphas, axis=0) * acc_ref[n] + pv

    n_rest = nkc - 1
    qk_chunk(n_rest, sa_ref)
    qk_chunk(0, sb_ref)
    attn_chunk(n_rest, sa_ref, True)

    def attn_body(i, carry):
        qk_chunk(2 * i + 1, sa_ref)
        attn_chunk(2 * i, sb_ref, False)
        qk_chunk(2 * i + 2, sb_ref)
        attn_chunk(2 * i + 1, sa_ref, False)
        return carry

    lax.fori_loop(0, n_rest // 2, attn_body, 0)

    @pl.when(n_rest % 2 == 1)
    def _():
        attn_chunk(n_rest - 1, sb_ref, False)

    for hd in range(N_HEADS):
        n, g = divmod(hd, GQA_G)
        a = acc_ref[n, g * tq:(g + 1) * tq, :]
        o_ref[0, :, hd * HEAD_DIM:(hd + 1) * HEAD_DIM] = (a[:, :HEAD_DIM] / a[:, HEAD_DIM:]).astype(bf16)


def _sparse_attn(q, qi, wi, k_all, v_all, ki_all, pos0, tq, ck, topk):
    B, T, _ = q.shape
    S = k_all.shape[1]
    nc = -(-S // ck)
    Sp = nc * ck
    tpc = ck // LANES
    assert ck % LANES == 0 and PLANE_BITS % tpc == 0 and T % tq == 0
    cpg = PLANE_BITS // tpc
    n_grp = -(-nc // cpg)
    assert Sp <= 256 * LANES
    for qb in range(T // tq):
        p_first = pos0 + qb * tq
        limit = min(S, ((p_first + tq - 1) // CHUNK + 1) * CHUNK)
        assert p_first >= (-(-limit // ck) - 1) * ck
    pad = lambda a: jnp.pad(a.astype(bf16), ((0, 0), (0, Sp - S), (0, 0)))
    j = jnp.arange(Sp)
    pos_rows = jnp.zeros((HEAD_DIM, Sp), f32).at[0].set(j // LANES).at[1].set(j % LANES).astype(bf16)
    kt = pad(k_all).reshape(B, Sp, N_KV, HEAD_DIM).transpose(0, 2, 3, 1)
    kt = jnp.concatenate([kt, jnp.broadcast_to(pos_rows, kt.shape)], axis=2)
    kt = kt.reshape(B, N_KV, 2 * HEAD_DIM, nc, ck).transpose(0, 3, 1, 2, 4)
    va = pad(v_all).reshape(B, Sp, N_KV, HEAD_DIM)
    va = jnp.concatenate([va, jnp.ones_like(va)], axis=-1).reshape(B, Sp, N_KV * 2 * HEAD_DIM)
    kit = pad(ki_all).transpose(0, 2, 1).reshape(B, IDX_DIM, nc, ck).transpose(0, 2, 1, 3)
    assert all((8.0 * (hd + 1) / N_HEADS).is_integer() for hd in range(N_HEADS))
    slopes = jnp.asarray([_slope(hd) for hd in range(N_HEADS)], f32)
    qs = jnp.zeros((N_HEADS, tq, HEAD_DIM), f32).at[:, :, 0].set(LANES * slopes[:, None])
    qs = qs.at[:, :, 1].set(slopes[:, None]).astype(bf16)

    kern = functools.partial(_sparse_attn_kernel, tq=tq, ck=ck, S=S, pos0=pos0, topk=topk)
    hq = N_HEADS * HEAD_DIM
    once = pl.Buffered(1)
    return pl.pallas_call(
        kern,
        grid=(B, T // tq),
        in_specs=[pl.BlockSpec((1, tq, hq), lambda b, t: (b, t, 0)),
                  pl.BlockSpec((N_HEADS, tq, HEAD_DIM), lambda b, t: (0, 0, 0), pipeline_mode=once),
                  pl.BlockSpec((1, tq, IDX_HEADS * IDX_DIM), lambda b, t: (b, t, 0)),
                  pl.BlockSpec((1, tq, IDX_HEADS), lambda b, t: (b, t, 0)),
                  pl.BlockSpec((1, nc, N_KV, 2 * HEAD_DIM, ck), lambda b, t: (b, 0, 0, 0, 0), pipeline_mode=once),
                  pl.BlockSpec((1, Sp, N_KV * 2 * HEAD_DIM), lambda b, t: (b, 0, 0), pipeline_mode=once),
                  pl.BlockSpec((1, nc, IDX_DIM, ck), lambda b, t: (b, 0, 0, 0), pipeline_mode=once)],
        out_specs=pl.BlockSpec((1, tq, hq), lambda b, t: (b, t, 0)),
        out_shape=jax.ShapeDtypeStruct((B, T, hq), bf16),
        scratch_shapes=[pltpu.VMEM((n_grp * cpg, tq, ck), jnp.int32),
                        pltpu.VMEM((n_grp, PLANE_BITS, tq, LANES), jnp.int32),
                        pltpu.VMEM((n_grp, tq, LANES), jnp.int32),
                        pltpu.VMEM((N_KV, GQA_G * tq, 2 * HEAD_DIM), bf16),
                        pltpu.VMEM((N_KV, GQA_G * tq, ck), f32),
                        pltpu.VMEM((N_KV, GQA_G * tq, ck), f32),
                        pltpu.VMEM((N_KV, GQA_G * tq, 1), f32),
                        pltpu.VMEM((N_KV, GQA_G * tq, 2 * HEAD_DIM), f32)],
        compiler_params=_cparams(2),
    )(q, qs, qi, wi, kt, va, kit)


SPAN = PLANE_BITS * SUBLANES
ATTN_TQ = 256
ONES_ROWS = 16


def _sparse_attn_t_kernel(qt_ref, qst_ref, qit_ref, wit_ref, k_ref, pos_ref, vt_ref, ki_ref, ot_ref,
                          key_ref, plane_ref, act_ref, qa_ref, sa_ref, sb_ref, mxa_ref, mxb_ref, m_ref, acc_ref,
                          *, tq, ck, S, pos0, topk):
    spc = ck // SPAN
    wr = spc * SUBLANES
    ltiles = tq // LANES
    qb = pl.program_id(1)
    p_first = pos0 + qb * tq
    limit = jnp.minimum(S, ((p_first + tq - 1) // CHUNK + 1) * CHUNK)
    nkc = (limit + ck - 1) // ck
    qpos = p_first + lax.broadcasted_iota(jnp.int32, (1, tq), 1)
    qchunk = qpos // CHUNK
    row_pos = lax.broadcasted_iota(jnp.int32, (ck, tq), 0)
    wit = wit_ref[0]

    for hd in range(N_HEADS):
        n, g = divmod(hd, GQA_G)
        qa_ref[n, 0:HEAD_DIM, g * tq:(g + 1) * tq] = qt_ref[0, hd * HEAD_DIM:(hd + 1) * HEAD_DIM, :]
        qa_ref[n, HEAD_DIM:2 * HEAD_DIM, g * tq:(g + 1) * tq] = qst_ref[hd]

    n_rest = nkc - 1

    def score_chunk(c, last):
        kic = ki_ref[0, pl.ds(pl.multiple_of(c * ck, ck), ck), :]
        sc = jnp.zeros((ck, tq), f32)
        for hd in range(IDX_HEADS):
            d = _mm(kic, qit_ref[0, hd * IDX_DIM:(hd + 1) * IDX_DIM, :])
            sc = sc + wit[hd:hd + 1, :] * jnp.maximum(d, 0.0)
        bits = pltpu.bitcast(sc, jnp.int32)
        key = bits ^ ((bits >> 31) & 0x7FFFFFFF)
        if last:
            kpos = c * ck + row_pos
            key = jnp.where((kpos // CHUNK <= qchunk) & (kpos < S), key, INT_MIN)
        key_ref[c] = key
        for u in range(spc):
            for lt in range(ltiles):
                w = [key[u * SPAN + k * SUBLANES:u * SPAN + (k + 1) * SUBLANES, lt * LANES:(lt + 1) * LANES]
                     for k in range(PLANE_BITS)]
                for j, msk in BUTTERFLY:
                    for k in range(PLANE_BITS):
                        if k & j == 0:
                            tmp = (w[k] ^ lax.shift_right_logical(w[k + j], jnp.int32(j))) & msk
                            w[k] = w[k] ^ tmp
                            w[k + j] = w[k + j] ^ (tmp << j)
                rows = pl.ds(pl.multiple_of(c * wr + u * SUBLANES, SUBLANES), SUBLANES)
                lanes = slice(lt * LANES, (lt + 1) * LANES)
                plane_ref[0, rows, lanes] = ~w[0]
                for i in range(1, PLANE_BITS):
                    plane_ref[i, rows, lanes] = w[i]
        act_ref[pl.ds(pl.multiple_of(c * wr, wr), wr), :] = jnp.full((wr, tq), -1, jnp.int32)

    def score_body(c, carry):
        score_chunk(c, False)
        return carry

    lax.fori_loop(0, n_rest, score_body, 0)
    score_chunk(n_rest, True)

    def chunk_rows(c):
        return pl.ds(pl.multiple_of(c * wr, wr), wr)

    def pair_body(i, carry):
        need, thr_u = carry

        def csum(c, parts):
            rows = chunk_rows(c)
            a, p1, p2 = act_ref[rows, :], plane_ref[2 * i, rows, :], plane_ref[2 * i + 1, rows, :]
            a1 = a & p1
            a11 = a1 & p2
            return (parts[0] + lax.population_count(a11), parts[1] + lax.population_count(a1 ^ a11),
                    parts[2] + lax.population_count((a ^ a1) & p2))

        zero = jnp.zeros((wr, tq), jnp.int32)
        parts = lax.fori_loop(0, nkc, csum, (zero, zero, zero))
        t1 = jnp.sum(parts[0], axis=0, keepdims=True)
        t2 = t1 + jnp.sum(parts[1], axis=0, keepdims=True)
        t3 = t2 + jnp.sum(parts[2], axis=0, keepdims=True)
        hi = need <= t2
        lo = (need <= t1) | (jnp.logical_not(hi) & (need <= t3))
        flip1 = jnp.where(hi, 0, -1)
        flip2 = jnp.where(lo, 0, -1)

        def cupd(c, c_):
            rows = chunk_rows(c)
            act_ref[rows, :] = (act_ref[rows, :] & (plane_ref[2 * i, rows, :] ^ flip1)
                                & (plane_ref[2 * i + 1, rows, :] ^ flip2))
            return c_

        lax.fori_loop(0, nkc, cupd, 0)
        need = need - jnp.where(hi, jnp.where(lo, 0, t1), jnp.where(lo, t2, t3))
        bits = jnp.where(hi, 2, 0) | jnp.where(lo, 1, 0)
        return need, thr_u | lax.shift_left(bits, 30 - 2 * i)

    need, thr_u = lax.fori_loop(0, PLANE_BITS // 2, pair_body,
                                (jnp.full((1, tq), topk, jnp.int32), jnp.zeros((1, tq), jnp.int32)))
    thr = thr_u ^ INT_MIN
    n_eq = jnp.sum(lax.fori_loop(0, nkc, lambda c, part: part + lax.population_count(act_ref[chunk_rows(c), :]),
                                 jnp.zeros((wr, tq), jnp.int32)), axis=0, keepdims=True)

    def count(pred):
        def body(c, part):
            return part + jnp.sum(pred(key_ref[c], c).astype(jnp.int32), axis=0, keepdims=True)
        return lax.fori_loop(0, nkc, body, jnp.zeros((1, tq), jnp.int32))

    real = thr != INT_MIN
    excess = real & (n_eq > need)
    pos_bits = max(1, (ck * key_ref.shape[0] - 1).bit_length())

    def tie_search():
        def pbody(i, ans):
            cand = ans | lax.shift_left(jnp.int32(1), pos_bits - 1 - i)
            cnt = count(lambda kc, c: (kc == thr) & (c * ck + row_pos < cand))
            return jnp.where(cnt < need, cand, ans)
        return lax.fori_loop(0, pos_bits, pbody, jnp.zeros((1, tq), jnp.int32))

    any_excess = jnp.max(excess.astype(jnp.int32)) > 0
    tie_lim = lax.cond(any_excess, tie_search, lambda: jnp.zeros((1, tq), jnp.int32))
    big = jnp.int32(ck * key_ref.shape[0])
    tie_lim = jnp.where(excess, tie_lim, jnp.where(real, big, -1))

    def qk_chunk(c, s_ref, mx_ref, first):
        rows = pl.ds(pl.multiple_of(c * ck, ck), ck)
        kc = key_ref[c]
        kpos = c * ck + row_pos
        sel = (kc > thr) | ((kc == thr) & (kpos <= tie_lim))
        bias = jnp.where(sel, 0.0, NEG)
        if first:
            ahead = jnp.maximum(kpos - qpos, 0).astype(f32)
        for n in range(N_KV):
            ka = jnp.concatenate([k_ref[0, rows, n * HEAD_DIM:(n + 1) * HEAD_DIM], pos_ref[rows, :]], axis=1)
            s = _mm(ka, qa_ref[n])
            for g in range(GQA_G):
                cols = slice(g * tq, (g + 1) * tq)
                sg = s[:, cols] + bias
                if first:
                    sg = sg - (2.0 * _slope(n * GQA_G + g)) * ahead
                s_ref[n, :, cols] = sg
                mx_ref[n, :, cols] = jnp.max(sg, axis=0, keepdims=True)

    def attn_chunk(c, s_ref, mx_ref, first):
        for n in range(N_KV):
            ps, alphas = [], []
            for g in range(GQA_G):
                cols = slice(g * tq, (g + 1) * tq)
                if first:
                    m_new = mx_ref[n, :, cols]
                else:
                    m_prev = m_ref[n, :, cols]
                    m_new = jnp.maximum(m_prev, mx_ref[n, :, cols])
                    alphas.append(jnp.exp(m_prev - m_new))
                m_ref[n, :, cols] = m_new
                ps.append(jnp.exp(s_ref[n, :, cols] - m_new).astype(bf16))
            vat = jnp.concatenate([vt_ref[0, c, n], jnp.ones((ONES_ROWS, ck), bf16)], axis=0)
            pv = _mm(vat, jnp.concatenate(ps, axis=1))
            acc_ref[n] = pv if first else jnp.concatenate(alphas, axis=1) * acc_ref[n] + pv

    qk_chunk(n_rest, sa_ref, mxa_ref, True)
    qk_chunk(0, sb_ref, mxb_ref, False)
    attn_chunk(n_rest, sa_ref, mxa_ref, True)

    def attn_body(i, carry):
        qk_chunk(2 * i + 1, sa_ref, mxa_ref, False)
        attn_chunk(2 * i, sb_ref, mxb_ref, False)
        qk_chunk(2 * i + 2, sb_ref, mxb_ref, False)
        attn_chunk(2 * i + 1, sa_ref, mxa_ref, False)
        return carry

    lax.fori_loop(0, n_rest // 2, attn_body, 0)

    @pl.when(n_rest % 2 == 1)
    def _():
        attn_chunk(n_rest - 1, sb_ref, mxb_ref, False)

    for hd in range(N_HEADS):
        n, g = divmod(hd, GQA_G)
        a = acc_ref[n, :, g * tq:(g + 1) * tq]
        ot_ref[0, hd * HEAD_DIM:(hd + 1) * HEAD_DIM, :] = (a[:HEAD_DIM] / a[HEAD_DIM:HEAD_DIM + 1]).astype(bf16)


def _sparse_attn_t(qt, qit, wit, k_all, v_all, ki_all, pos0, tq, ck, topk):
    B, _, T = qt.shape
    S = k_all.shape[1]
    nc = -(-S // ck)
    Sp = nc * ck
    assert ck % SPAN == 0 and tq % LANES == 0 and T % tq == 0
    assert Sp <= 256 * LANES
    for qb in range(T // tq):
        p_first = pos0 + qb * tq
        limit = min(S, ((p_first + tq - 1) // CHUNK + 1) * CHUNK)
        assert p_first >= (-(-limit // ck) - 1) * ck
    pad = lambda a: jnp.pad(a.astype(bf16), ((0, 0), (0, Sp - S), (0, 0)))
    j = jnp.arange(Sp)
    pos_cols = jnp.zeros((Sp, HEAD_DIM), f32).at[:, 0].set(j // LANES).at[:, 1].set(j % LANES).astype(bf16)
    vt = pad(v_all).reshape(B, Sp, N_KV, HEAD_DIM).transpose(0, 2, 3, 1)
    vt = vt.reshape(B, N_KV, HEAD_DIM, nc, ck).transpose(0, 3, 1, 2, 4)
    assert all((8.0 * (hd + 1) / N_HEADS).is_integer() for hd in range(N_HEADS))
    slopes = jnp.asarray([_slope(hd) for hd in range(N_HEADS)], f32)
    qst = jnp.zeros((N_HEADS, HEAD_DIM, tq), f32).at[:, 0, :].set(LANES * slopes[:, None])
    qst = qst.at[:, 1, :].set(slopes[:, None]).astype(bf16)

    kern = functools.partial(_sparse_attn_t_kernel, tq=tq, ck=ck, S=S, pos0=pos0, topk=topk)
    hq = N_HEADS * HEAD_DIM
    once = pl.Buffered(1)
    wr = ck // SPAN * SUBLANES
    return pl.pallas_call(
        kern,
        grid=(B, T // tq),
        in_specs=[pl.BlockSpec((1, hq, tq), lambda b, t: (b, 0, t)),
                  pl.BlockSpec((N_HEADS, HEAD_DIM, tq), lambda b, t: (0, 0, 0), pipeline_mode=once),
                  pl.BlockSpec((1, IDX_HEADS * IDX_DIM, tq), lambda b, t: (b, 0, t)),
                  pl.BlockSpec((1, IDX_HEADS, tq), lambda b, t: (b, 0, t)),
                  pl.BlockSpec((1, Sp, N_KV * HEAD_DIM), lambda b, t: (b, 0, 0), pipeline_mode=once),
                  pl.BlockSpec((Sp, HEAD_DIM), lambda b, t: (0, 0), pipeline_mode=once),
                  pl.BlockSpec((1, nc, N_KV, HEAD_DIM, ck), lambda b, t: (b, 0, 0, 0, 0), pipeline_mode=once),
                  pl.BlockSpec((1, Sp, IDX_DIM), lambda b, t: (b, 0, 0), pipeline_mode=once)],
        out_specs=pl.BlockSpec((1, hq, tq), lambda b, t: (b, 0, t)),
        out_shape=jax.ShapeDtypeStruct((B, hq, T), bf16),
        scratch_shapes=[pltpu.VMEM((nc, ck, tq), jnp.int32),
                        pltpu.VMEM((PLANE_BITS, nc * wr, tq), jnp.int32),
                        pltpu.VMEM((nc * wr, tq), jnp.int32),
                        pltpu.VMEM((N_KV, 2 * HEAD_DIM, GQA_G * tq), bf16),
                        pltpu.VMEM((N_KV, ck, GQA_G * tq), f32),
                        pltpu.VMEM((N_KV, ck, GQA_G * tq), f32),
                        pltpu.VMEM((N_KV, 1, GQA_G * tq), f32),
                        pltpu.VMEM((N_KV, 1, GQA_G * tq), f32),
                        pltpu.VMEM((N_KV, 1, GQA_G * tq), f32),
                        pltpu.VMEM((N_KV, HEAD_DIM + ONES_ROWS, GQA_G * tq), f32)],
        compiler_params=_cparams(2),
    )(qt, qst, qit, wit, pad(k_all), pos_cols, vt, pad(ki_all))


def _merge_kernel(x_ref, h_ref, ya_ref, yb_ref, o_ref, g1_ref, wgl_ref, wpc_ref, wo_ref,
                  gn2_ref, sc2_ref, sh2_ref, x1_ref, h2_ref, *, bb, tt, o_on_lanes):
    D = D_MODEL
    M = bb * tt
    h = h_ref[...].reshape(M, D)
    if o_on_lanes:
        yc = lax.dot_general(o_ref[0], wpc_ref[...], (((0,), (0,)), ((), ())), preferred_element_type=f32)
    else:
        yc = _mm(o_ref[...].reshape(M, D), wpc_ref[...])
    m = _sigmoid(_mm(h, wgl_ref[:, 0:D])) * ya_ref[...].reshape(M, D).astype(f32)
    m = m + _sigmoid(_mm(h, wgl_ref[:, D:2 * D])) * yb_ref[...].reshape(M, D).astype(f32)
    m = m + _sigmoid(_mm(h, wgl_ref[:, 2 * D:3 * D])) * yc
    x1 = x_ref[...] + g1_ref[...] * _mm(m.astype(bf16), wo_ref[...]).reshape(bb, tt, D)
    x1_ref[...] = x1
    h2 = _rms(x1, gn2_ref[...]) * (1.0 + sc2_ref[...]) + sh2_ref[...]
    h2_ref[...] = h2.astype(bf16)


def _merge(x, h, ya, yb, o, g1, wgl, wpc, wo, gn2, sc2, sh2, o_on_lanes):
    B, T, D = x.shape
    bb, tt = _tiles(B, T)
    assert bb == 1 or not o_on_lanes
    kern = functools.partial(_merge_kernel, bb=bb, tt=tt, o_on_lanes=o_on_lanes)
    row = _row_spec(bb, tt, D)
    vec = _batch_spec(bb, 1, D)
    o_spec = pl.BlockSpec((1, D, tt), lambda b, t: (b, 0, t)) if o_on_lanes else row
    return pl.pallas_call(
        kern,
        grid=(B // bb, T // tt),
        in_specs=[row, row, row, row, o_spec, vec, _const_spec((D, 3 * D)), _const_spec((D, D)),
                  _const_spec((D, D)), _const_spec((1, 1, D)), vec, vec],
        out_specs=[row, row],
        out_shape=[jax.ShapeDtypeStruct((B, T, D), f32), jax.ShapeDtypeStruct((B, T, D), bf16)],
        compiler_params=_cparams(2),
    )(x, h, ya, yb, o, g1, wgl, wpc, wo, gn2.reshape(1, 1, D), sc2, sh2)


FF_TILE = 1024


def _mlp_kernel(x1_ref, h2_ref, g2_ref, w1_ref, b1_ref, w2_ref, b2_ref, x2_ref, *, bb, tt):
    D = D_MODEL
    M = bb * tt
    h2 = h2_ref[...].reshape(M, D)
    acc = jnp.zeros((M, D), f32)
    for c0 in range(0, D_FF, FF_TILE):
        f = jnp.maximum(_mm(h2, w1_ref[:, c0:c0 + FF_TILE]) + b1_ref[:, c0:c0 + FF_TILE], 0.0)
        acc = acc + _mm((f * f).astype(bf16), w2_ref[c0:c0 + FF_TILE, :])
    f = (acc + b2_ref[...]).reshape(bb, tt, D)
    x2_ref[...] = x1_ref[...] + g2_ref[...] * f


def _mlp(x1, h2, g2, w1, b1, w2, b2):
    B, T, D = x1.shape
    bb, tt = _tiles(B, T)
    kern = functools.partial(_mlp_kernel, bb=bb, tt=tt)
    row = _row_spec(bb, tt, D)
    return pl.pallas_call(
        kern,
        grid=(B // bb, T // tt),
        in_specs=[row, row, _batch_spec(bb, 1, D), _const_spec((D, D_FF)), _const_spec((1, D_FF)),
                  _const_spec((D_FF, D)), _const_spec((1, D))],
        out_specs=row,
        out_shape=jax.ShapeDtypeStruct((B, T, D), f32),
        compiler_params=_cparams(2),
    )(x1, h2, g2, w1, b1.reshape(1, D_FF), w2, b2.reshape(1, D))


def _split_w_in(w_in):
    D = D_MODEL
    w = w_in.astype(bf16)
    o_b = 2 * D
    o_q = 4 * D
    o_gl = o_q + N_HEADS * HEAD_DIM + 2 * N_KV * HEAD_DIM + IDX_HEADS * IDX_DIM + IDX_DIM + IDX_HEADS
    n_c = o_gl - o_q
    w_c = jnp.pad(w[:, o_q:o_gl], ((0, 0), (0, QKV_COLS - n_c)))
    return dict(wa1=w[:, 0:D], wa2=w[:, D:2 * D], wbu=w[:, o_b:o_b + D], wbv=w[:, o_b + D:o_b + 2 * D],
                w_c=w_c, wgl=w[:, o_gl:o_gl + 3 * D])


def _layer(x, mod, cache, W, pos0):
    B, T, D = x.shape
    sh1, sc1, g1, sh2, sc2, g2 = mod
    win = W['w_in']
    h = _prenorm(x, W['g_norm1'], sc1, sh1)

    hist = jnp.zeros((B, HIST, D), f32) if cache is None else cache[3]
    ya, new_conv = _branch_conv(h, hist, win['wa1'], win['wa2'], W['w_dw'], W['b_dw'],
                                W['ln_c_g'], W['ln_c_b'], W['w_pa'])

    L = GMLP_CHUNK if cache is None else T
    res = _branch_gmlp(h, win['wbu'], win['wbv'], W['ln_v_g'], W['ln_v_b'], W['w_s'], W['b_s'],
                       W['w_pb'], L, cache is not None)
    yb = res[0]
    vg = res[1] if cache is not None else None

    on_lanes = T % ATTN_TQ == 0
    q, k, v, qi, ki, wi = _attn_proj(h, win['w_c'], W['g_q'], W['g_k'], on_lanes)
    if cache is None:
        k_all, v_all, ki_all = k, v, ki
        ck = 512
    else:
        k_all = jnp.concatenate([cache[0].reshape(B, -1, N_KV * HEAD_DIM), k], axis=1)
        v_all = jnp.concatenate([cache[1].reshape(B, -1, N_KV * HEAD_DIM), v], axis=1)
        ki_all = jnp.concatenate([cache[2], ki], axis=1)
        ck = 256
    topk = min(TOPK_MAX, k_all.shape[1] // 4)
    if on_lanes:
        o = _sparse_attn_t(q, qi, wi, k_all, v_all, ki_all, pos0, ATTN_TQ, ck, topk)
    else:
        o = _sparse_attn(q, qi, wi, k_all, v_all, ki_all, pos0, min(Q_BLOCK, T), ck, topk)

    x1, h2 = _merge(x, h, ya, yb, o, g1, win['wgl'], W['w_pc'], W['w_o'], W['g_norm2'], sc2, sh2, on_lanes)
    x2 = _mlp(x1, h2, g2, W['w_1'], W['b_1'], W['w_2'], W['b_2'])
    return x2, (k.reshape(B, T, N_KV, HEAD_DIM), v.reshape(B, T, N_KV, HEAD_DIM), ki, new_conv, vg)


def kernel(x_prompt, x_sample, c_prompt, c_sample, cache_k, cache_v, cache_kidx, cache_conv, w_ada, b_ada, g_norm1, w_in, w_dw, b_dw, ln_c_g, ln_c_b, w_pa, ln_v_g, ln_v_b, w_s, b_s, w_pb, g_q, g_k, w_pc, w_o, g_norm2, w_1, b_1, w_2, b_2):
    depth = w_in.shape[0]
    D = D_MODEL
    Bp, Bs = x_prompt.shape[0], x_sample.shape[0]
    past = cache_k.shape[2]
    c_all = jnp.concatenate([c_prompt, c_sample], axis=0)
    n_c = c_all.shape[0]
    n_pad = -(-n_c // SUBLANES) * SUBLANES
    mod_all = _ada(jnp.pad(c_all, ((0, n_pad - n_c), (0, 0))), w_ada.astype(bf16), b_ada)

    def mods(l, lo, n):
        m = mod_all[l, lo:lo + n]
        return [m[:, None, i * D:(i + 1) * D] for i in range(6)]

    hp, hs = x_prompt, x_sample
    outs_p, outs_s = [], []
    for l in range(depth):
        W = dict(g_norm1=g_norm1[l], w_in=_split_w_in(w_in[l]), w_dw=w_dw[l], b_dw=b_dw[l],
                 ln_c_g=ln_c_g[l], ln_c_b=ln_c_b[l], w_pa=w_pa[l].astype(bf16), ln_v_g=ln_v_g[l],
                 ln_v_b=ln_v_b[l], w_s=w_s[l], b_s=b_s[l], w_pb=w_pb[l].astype(bf16), g_q=g_q[l],
                 g_k=g_k[l], w_pc=w_pc[l].astype(bf16), w_o=w_o[l].astype(bf16), g_norm2=g_norm2[l],
                 w_1=w_1[l].astype(bf16), b_1=b_1[l], w_2=w_2[l].astype(bf16), b_2=b_2[l])
        hp, op = _layer(hp, mods(l, 0, Bp), None, W, 0)
        hs, os_ = _layer(hs, mods(l, Bp, Bs), (cache_k[l], cache_v[l], cache_kidx[l], cache_conv[l]), W, past)
        outs_p.append(op)
        outs_s.append(os_)
    stack = lambda outs, i: jnp.stack([o[i] for o in outs])
    return (hp, hs,
            stack(outs_p, 0), stack(outs_p, 1), stack(outs_p, 2), stack(outs_p, 3),
            stack(outs_s, 0), stack(outs_s, 1), stack(outs_s, 2), stack(outs_s, 3), stack(outs_s, 4))
```

```python
import functools

import jax
import jax.numpy as jnp
from jax import lax
from jax.experimental import pallas as pl
from jax.experimental.pallas import tpu as pltpu

D_MODEL = 1024
CHUNK = 64
CONV_W = 31
HIST = CONV_W - 1
GMLP_CHUNK = 128
GMLP_GROUPS = 8
GMLP_GW = D_MODEL // GMLP_GROUPS
N_HEADS = 8
HEAD_DIM = 128
N_KV = 2
GQA_G = N_HEADS // N_KV
IDX_HEADS = 8
IDX_DIM = 64
TOPK_MAX = 256
Q_BLOCK = 128
D_FF = 4 * D_MODEL
EPS = 1e-6

LANES = 128
SUBLANES = 8
HALO = 32
ROW_TILE = 512
VMEM_LIMIT = 56 * 1024 * 1024
INT_MIN = -2 ** 31
NEG = -1e30

f32 = jnp.float32
bf16 = jnp.bfloat16


def _cparams(n_axes):
    return pltpu.CompilerParams(dimension_semantics=("arbitrary",) * n_axes,
                                vmem_limit_bytes=VMEM_LIMIT)


def _tiles(B, T):
    if T >= ROW_TILE:
        assert T % ROW_TILE == 0
        return 1, ROW_TILE
    assert ROW_TILE % T == 0 and B % (ROW_TILE // T) == 0
    return ROW_TILE // T, T


def _row_spec(bb, tt, n):
    return pl.BlockSpec((bb, tt, n), lambda b, t: (b, t, 0))


def _batch_spec(bb, r, n):
    return pl.BlockSpec((bb, r, n), lambda b, t: (b, 0, 0))


def _const_spec(shape):
    nd = len(shape)
    return pl.BlockSpec(shape, lambda b, t: (0,) * nd, pipeline_mode=pl.Buffered(1))


def _mm(a, w):
    return jnp.dot(a, w, preferred_element_type=f32)


def _sigmoid(x):
    return 1.0 / (1.0 + jnp.exp(-x))


def _silu(x):
    return x * _sigmoid(x)


def _gelu_tanh(x):
    return 0.5 * x * (1.0 + jnp.tanh(0.7978845608028654 * (x + 0.044715 * (x * x * x))))


def _rms(x, g):
    return x * lax.rsqrt(jnp.mean(x * x, axis=-1, keepdims=True) + EPS) * g


def _ln(x, g, b):
    mu = jnp.mean(x, axis=-1, keepdims=True)
    xc = x - mu
    var = jnp.mean(xc * xc, axis=-1, keepdims=True)
    return xc * lax.rsqrt(var + EPS) * g + b


def _ada_kernel(c_ref, w_ref, b_ref, o_ref):
    c = c_ref[...]
    o_ref[0] = _mm(_silu(c).astype(bf16), w_ref[0]) + b_ref[0]


def _ada(c_all, w_ada, b_ada):
    depth, d, n = w_ada.shape
    bp = c_all.shape[0]
    tn = 1024
    return pl.pallas_call(
        _ada_kernel,
        grid=(depth, n // tn),
        in_specs=[pl.BlockSpec((bp, d), lambda l, j: (0, 0)),
                  pl.BlockSpec((1, d, tn), lambda l, j: (l, 0, j)),
                  pl.BlockSpec((1, 1, tn), lambda l, j: (l, 0, j))],
        out_specs=pl.BlockSpec((1, bp, tn), lambda l, j: (l, 0, j)),
        out_shape=jax.ShapeDtypeStruct((depth, bp, n), f32),
        compiler_params=_cparams(2),
    )(c_all, w_ada, b_ada.reshape(depth, 1, n))


def _prenorm_kernel(x_ref, g_ref, sc_ref, sh_ref, h_ref):
    h = _rms(x_ref[...], g_ref[...]) * (1.0 + sc_ref[...]) + sh_ref[...]
    h_ref[...] = h.astype(bf16)


def _prenorm(x, g, sc, sh):
    B, T, D = x.shape
    bb, tt = _tiles(B, T)
    return pl.pallas_call(
        _prenorm_kernel,
        grid=(B // bb, T // tt),
        in_specs=[_row_spec(bb, tt, D), _const_spec((1, 1, D)),
                  _batch_spec(bb, 1, D), _batch_spec(bb, 1, D)],
        out_specs=_row_spec(bb, tt, D),
        out_shape=jax.ShapeDtypeStruct((B, T, D), bf16),
        compiler_params=_cparams(2),
    )(x, g.reshape(1, 1, D), sc, sh)


CONV_ROWS = 64
CONV_COLS = 256


def _conv_kernel(h_ref, hist_ref, wa1_ref, wa2_ref, wdw_ref, bdw_ref, lng_ref, lnb_ref, wpa_ref,
                 ya_ref, newc_ref, apad_ref, conv_ref, shift_ref, *, bb, tt):
    D = D_MODEL
    t = pl.program_id(1)

    @pl.when(t == 0)
    def _():
        apad_ref[:, 0:HALO - HIST, :] = jnp.zeros((bb, HALO - HIST, D), f32)
        apad_ref[:, HALO - HIST:HALO, :] = hist_ref[...]

    h = h_ref[...].reshape(bb * tt, D)
    a = _mm(h, wa1_ref[...]) * _sigmoid(_mm(h, wa2_ref[...]))
    apad_ref[:, HALO:HALO + tt, :] = a.reshape(bb, tt, D)

    base = HALO - HIST
    n_sh = HALO + tt - SUBLANES
    for b in range(bb):
        for c0 in range(0, D, CONV_COLS):
            cols = slice(c0, c0 + CONV_COLS)
            for s in range(1, SUBLANES):
                shift_ref[s - 1, 0:n_sh, :] = apad_ref[b, s:s + n_sh, cols]
            for r0 in range(0, tt, CONV_ROWS):
                acc = jnp.zeros((CONV_ROWS, CONV_COLS), f32)
                for j in range(CONV_W):
                    q8, s = divmod(base + j, SUBLANES)
                    lo = r0 + SUBLANES * q8
                    src = (apad_ref[b, lo:lo + CONV_ROWS, cols] if s == 0
                           else shift_ref[s - 1, lo:lo + CONV_ROWS, :])
                    acc = acc + wdw_ref[j:j + 1, cols] * src
                conv_ref[b * tt + r0:b * tt + r0 + CONV_ROWS, cols] = acc

    tail = apad_ref[:, HALO + tt - HIST:HALO + tt, :]
    newc_ref[...] = tail
    apad_ref[:, HALO - HIST:HALO, :] = tail

    y = conv_ref[...] + bdw_ref[...]
    z = _silu(_ln(y, lng_ref[...], lnb_ref[...]))
    ya_ref[...] = _mm(z.astype(bf16), wpa_ref[...]).astype(bf16).reshape(bb, tt, D)


def _branch_conv(h, hist, wa1, wa2, wdw, bdw, lng, lnb, wpa):
    B, T, D = h.shape
    bb, tt = _tiles(B, T)
    kern = functools.partial(_conv_kernel, bb=bb, tt=tt)
    return pl.pallas_call(
        kern,
        grid=(B // bb, T // tt),
        in_specs=[_row_spec(bb, tt, D), _batch_spec(bb, HIST, D),
                  _const_spec((D, D)), _const_spec((D, D)), _const_spec((CONV_W, D)),
                  _const_spec((1, D)), _const_spec((1, D)), _const_spec((1, D)), _const_spec((D, D))],
        out_specs=[_row_spec(bb, tt, D), _batch_spec(bb, HIST, D)],
        out_shape=[jax.ShapeDtypeStruct((B, T, D), bf16), jax.ShapeDtypeStruct((B, HIST, D), f32)],
        scratch_shapes=[pltpu.VMEM((bb, HALO + tt, D), f32), pltpu.VMEM((bb * tt, D), f32),
                        pltpu.VMEM((SUBLANES - 1, HALO + tt, CONV_COLS), f32)],
        compiler_params=_cparams(2),
    )(h, hist, wa1, wa2, wdw, bdw.reshape(1, D), lng.reshape(1, D), lnb.reshape(1, D), wpa)


def _gmlp_kernel(h_ref, wbu_ref, wbv_ref, lng_ref, lnb_ref, wm_ref, bs_ref, wpb_ref,
                 yb_ref, *rest, bb, tt, L, emit_v):
    D = D_MODEL
    if emit_v:
        vg_ref, s_ref = rest
    else:
        (s_ref,) = rest
    M = bb * tt
    h = h_ref[...].reshape(M, D)
    u = _gelu_tanh(_mm(h, wbu_ref[...]))
    vg = _ln(_gelu_tanh(_mm(h, wbv_ref[...])), lng_ref[...], lnb_ref[...])
    if emit_v:
        vg_ref[...] = vg.reshape(bb, tt, D)
    vgb = vg.astype(bf16)
    for c in range(M // L):
        for g in range(GMLP_GROUPS):
            blk = vgb[c * L:(c + 1) * L, g * GMLP_GW:(g + 1) * GMLP_GW]
            s_ref[c * L:(c + 1) * L, g * GMLP_GW:(g + 1) * GMLP_GW] = _mm(wm_ref[g], blk) + bs_ref[:, g * GMLP_GW:(g + 1) * GMLP_GW]
    yb_ref[...] = _mm((u * s_ref[...]).astype(bf16), wpb_ref[...]).astype(bf16).reshape(bb, tt, D)


def _branch_gmlp(h, wbu, wbv, lng, lnb, w_s, b_s, wpb, L, emit_v):
    B, T, D = h.shape
    bb, tt = _tiles(B, T)
    assert tt % L == 0
    tril = jnp.tril(jnp.ones((L, L), dtype=bool))
    wm = jnp.where(tril[None], w_s[:, :L, :L], 0).astype(bf16)
    bs = jnp.repeat(jnp.swapaxes(b_s[:, :L], 0, 1), GMLP_GW, axis=1)
    kern = functools.partial(_gmlp_kernel, bb=bb, tt=tt, L=L, emit_v=emit_v)
    out_specs = [_row_spec(bb, tt, D)]
    out_shape = [jax.ShapeDtypeStruct((B, T, D), bf16)]
    if emit_v:
        out_specs.append(_row_spec(bb, tt, D))
        out_shape.append(jax.ShapeDtypeStruct((B, T, D), f32))
    return pl.pallas_call(
        kern,
        grid=(B // bb, T // tt),
        in_specs=[_row_spec(bb, tt, D), _const_spec((D, D)), _const_spec((D, D)),
                  _const_spec((1, D)), _const_spec((1, D)), _const_spec((GMLP_GROUPS, L, L)),
                  _const_spec((L, D)), _const_spec((D, D))],
        out_specs=out_specs,
        out_shape=out_shape,
        scratch_shapes=[pltpu.VMEM((bb * tt, D), f32)],
        compiler_params=_cparams(2),
    )(h, wbu, wbv, lng.reshape(1, D), lnb.reshape(1, D), wm, bs, wpb)


QKV_COLS = N_HEADS * HEAD_DIM + 2 * N_KV * HEAD_DIM + IDX_HEADS * IDX_DIM + LANES


def _attn_proj_kernel(h_ref, w_ref, gq_ref, gk_ref, q_ref, k_ref, v_ref, qi_ref, ki_ref, wi_ref, *rest,
                      bb, tt, queries_on_lanes):
    D = D_MODEL
    M = bb * tt
    p = _mm(h_ref[...].reshape(M, D), w_ref[...])
    if queries_on_lanes:
        kb_ref, vt_ref, kib_ref = rest
    o = 0
    for hd in range(N_HEADS):
        qh = _rms(p[:, o:o + HEAD_DIM], gq_ref[...]) * (HEAD_DIM ** -0.5 * LOG2E)
        if queries_on_lanes:
            q_ref[0, hd * HEAD_DIM:(hd + 1) * HEAD_DIM, :] = qh.T.astype(bf16)
        else:
            q_ref[:, :, hd * HEAD_DIM:(hd + 1) * HEAD_DIM] = qh.astype(bf16).reshape(bb, tt, HEAD_DIM)
        o += HEAD_DIM
    for n in range(N_KV):
        kn = _rms(p[:, o:o + HEAD_DIM], gk_ref[...])
        k_ref[:, :, n * HEAD_DIM:(n + 1) * HEAD_DIM] = kn.reshape(bb, tt, HEAD_DIM)
        if queries_on_lanes:
            kb_ref[0, :, n * HEAD_DIM:(n + 1) * HEAD_DIM] = kn.astype(bf16)
        o += HEAD_DIM
    v_ref[...] = p[:, o:o + N_KV * HEAD_DIM].reshape(bb, tt, N_KV * HEAD_DIM)
    if queries_on_lanes:
        for n in range(N_KV):
            vt_ref[0, 0, n] = p[:, o + n * HEAD_DIM:o + (n + 1) * HEAD_DIM].T.astype(bf16)
    o += N_KV * HEAD_DIM
    if queries_on_lanes:
        for c0 in range(0, IDX_HEADS * IDX_DIM, LANES):
            qi_ref[0, c0:c0 + LANES, :] = p[:, o + c0:o + c0 + LANES].T.astype(bf16)
    else:
        qi_ref[...] = p[:, o:o + IDX_HEADS * IDX_DIM].astype(bf16).reshape(bb, tt, IDX_HEADS * IDX_DIM)
    o += IDX_HEADS * IDX_DIM
    ki_ref[...] = p[:, o:o + IDX_DIM].reshape(bb, tt, IDX_DIM)
    if queries_on_lanes:
        kib_ref[0] = p[:, o:o + IDX_DIM].astype(bf16)
    wi_scale = IDX_HEADS ** -0.5 * IDX_DIM ** -0.5
    if queries_on_lanes:
        tail = p[:, o:o + LANES].T
        wi_ref[0] = tail[IDX_DIM:IDX_DIM + IDX_HEADS, :] * wi_scale
    else:
        wi_ref[...] = (p[:, o + IDX_DIM:o + IDX_DIM + IDX_HEADS] * wi_scale).reshape(bb, tt, IDX_HEADS)


def _attn_proj(h, w_c, g_q, g_k, queries_on_lanes):
    B, T, D = h.shape
    bb, tt = _tiles(B, T)
    assert bb == 1 or not queries_on_lanes
    kern = functools.partial(_attn_proj_kernel, bb=bb, tt=tt, queries_on_lanes=queries_on_lanes)
    widths = (N_HEADS * HEAD_DIM, N_KV * HEAD_DIM, N_KV * HEAD_DIM, IDX_HEADS * IDX_DIM, IDX_DIM, IDX_HEADS)
    dtypes = (bf16, f32, f32, bf16, f32, f32)
    flipped = (queries_on_lanes, False, False, queries_on_lanes, False, queries_on_lanes)
    col_spec = lambda n: pl.BlockSpec((1, n, tt), lambda b, t: (b, 0, t))
    out_specs = [col_spec(n) if fl else _row_spec(bb, tt, n) for n, fl in zip(widths, flipped)]
    out_shape = [jax.ShapeDtypeStruct((B, n, T) if fl else (B, T, n), dt)
                 for n, dt, fl in zip(widths, dtypes, flipped)]
    if queries_on_lanes:
        out_specs += [_row_spec(1, tt, N_KV * HEAD_DIM),
                      pl.BlockSpec((1, 1, N_KV, HEAD_DIM, tt), lambda b, t: (b, t, 0, 0, 0)),
                      _row_spec(1, tt, IDX_DIM)]
        out_shape += [jax.ShapeDtypeStruct((B, T, N_KV * HEAD_DIM), bf16),
                      jax.ShapeDtypeStruct((B, T // tt, N_KV, HEAD_DIM, tt), bf16),
                      jax.ShapeDtypeStruct((B, T, IDX_DIM), bf16)]
    return pl.pallas_call(
        kern,
        grid=(B // bb, T // tt),
        in_specs=[_row_spec(bb, tt, D), _const_spec((D, QKV_COLS)),
                  _const_spec((1, HEAD_DIM)), _const_spec((1, HEAD_DIM))],
        out_specs=out_specs,
        out_shape=out_shape,
        compiler_params=_cparams(2),
    )(h, w_c, g_q.reshape(1, HEAD_DIM), g_k.reshape(1, HEAD_DIM))


PLANE_BITS = 32
BUTTERFLY = ((16, 0x0000FFFF), (8, 0x00FF00FF), (4, 0x0F0F0F0F), (2, 0x33333333), (1, 0x55555555))


LOG2E = 1.4426950408889634
N_ALIBI = 6


def _slope(hd):
    return 2.0 ** (-8.0 * (hd + 1) / N_HEADS)


def _alibi_operands(n_keys):
    c = jnp.asarray([_slope(hd) * LOG2E for hd in range(N_HEADS)], f32)
    hi = c.astype(bf16).astype(f32)
    mid = (c - hi).astype(bf16).astype(f32)
    lo = (c - hi - mid).astype(bf16).astype(f32)
    parts = jnp.stack([hi, mid, lo], axis=1)
    q_side = jnp.concatenate([LANES * parts, parts], axis=1).astype(bf16)
    j = jnp.arange(n_keys)
    a, b = (j // LANES).astype(f32), (j % LANES).astype(f32)
    k_side = jnp.stack([a, a, a, b, b, b], axis=1).astype(bf16)
    return q_side, k_side


def _sparse_attn_kernel(q_ref, qs_ref, qi_ref, wi_ref, kt_ref, va_ref, kit_ref, o_ref,
                        key_ref, plane_ref, act_ref, qa_ref, sa_ref, sb_ref, m_ref, acc_ref,
                        *, tq, ck, S, pos0, topk):
    tpc = ck // LANES
    cpg = PLANE_BITS // tpc
    rgrp = tq // SUBLANES
    qb = pl.program_id(1)
    p_first = pos0 + qb * tq
    limit = jnp.minimum(S, ((p_first + tq - 1) // CHUNK + 1) * CHUNK)
    nkc = (limit + ck - 1) // ck
    ngrp = (nkc + cpg - 1) // cpg
    qpos = p_first + lax.broadcasted_iota(jnp.int32, (tq, 1), 0)
    qchunk = qpos // CHUNK
    lane_pos = lax.broadcasted_iota(jnp.int32, (tq, ck), 1)
    wi = wi_ref[0]
    qi = qi_ref[0]
    qih = [qi[:, hd * IDX_DIM:(hd + 1) * IDX_DIM] for hd in range(IDX_HEADS)]

    for hd in range(N_HEADS):
        n, g = divmod(hd, GQA_G)
        qa_ref[n, g * tq:(g + 1) * tq, 0:HEAD_DIM] = q_ref[0, :, hd * HEAD_DIM:(hd + 1) * HEAD_DIM]
        qa_ref[n, g * tq:(g + 1) * tq, HEAD_DIM:2 * HEAD_DIM] = qs_ref[hd]

    def score_body(c, carry):
        kic = kit_ref[0, c]
        sc = jnp.zeros((tq, ck), f32)
        for hd in range(IDX_HEADS):
            sc = sc + wi[:, hd:hd + 1] * jnp.maximum(_mm(qih[hd], kic), 0.0)
        kpos = c * ck + lane_pos
        vis = (kpos // CHUNK <= qchunk) & (kpos < S)
        bits = pltpu.bitcast(sc, jnp.int32)
        key = bits ^ ((bits >> 31) & 0x7FFFFFFF)
        key_ref[c] = jnp.where(vis, key, INT_MIN)
        return carry

    lax.fori_loop(0, nkc, score_body, 0)

    def fill_body(c, carry):
        key_ref[c] = jnp.full((tq, ck), INT_MIN, jnp.int32)
        return carry

    lax.fori_loop(nkc, ngrp * cpg, fill_body, 0)

    def xpose_body(it, carry):
        g = it // rgrp
        rows = pl.ds(pl.multiple_of((it % rgrp) * SUBLANES, SUBLANES), SUBLANES)
        w = [key_ref[g * cpg + t // tpc, rows, (t % tpc) * LANES:(t % tpc + 1) * LANES]
             for t in range(PLANE_BITS)]
        for j, msk in BUTTERFLY:
            for k in range(PLANE_BITS):
                if k & j == 0:
                    tmp = (w[k] ^ lax.shift_right_logical(w[k + j], jnp.int32(j))) & msk
                    w[k] = w[k] ^ tmp
                    w[k + j] = w[k + j] ^ (tmp << j)
        plane_ref[g, 0, rows, :] = ~w[0]
        for i in range(1, PLANE_BITS):
            plane_ref[g, i, rows, :] = w[i]
        act_ref[g, rows, :] = jnp.full((SUBLANES, LANES), -1, jnp.int32)
        return carry

    lax.fori_loop(0, ngrp * rgrp, xpose_body, 0)

    def pair_body(i, carry):
        need, thr_u = carry

        def gsum(g, parts):
            a, p1, p2 = act_ref[g], plane_ref[g, 2 * i], plane_ref[g, 2 * i + 1]
            a1, a0 = a & p1, a & ~p1
            return (parts[0] + lax.population_count(a1 & p2), parts[1] + lax.population_count(a1 & ~p2),
                    parts[2] + lax.population_count(a0 & p2))

        zero = jnp.zeros((tq, LANES), jnp.int32)
        parts = lax.fori_loop(0, ngrp, gsum, (zero, zero, zero))
        cnt = jnp.sum(jnp.concatenate(parts, axis=0), axis=1, keepdims=True)
        t1 = cnt[0:tq]
        t2 = t1 + cnt[tq:2 * tq]
        t3 = t2 + cnt[2 * tq:3 * tq]
        hi = need <= t2
        lo = (need <= t1) | (jnp.logical_not(hi) & (need <= t3))
        flip1 = jnp.where(hi, 0, -1)
        flip2 = jnp.where(lo, 0, -1)

        def gupd(g, c_):
            act_ref[g] = act_ref[g] & (plane_ref[g, 2 * i] ^ flip1) & (plane_ref[g, 2 * i + 1] ^ flip2)
            return c_

        lax.fori_loop(0, ngrp, gupd, 0)
        need = need - jnp.where(hi, jnp.where(lo, 0, t1), jnp.where(lo, t2, t3))
        bits = jnp.where(hi, 2, 0) | jnp.where(lo, 1, 0)
        return need, thr_u | lax.shift_left(bits, 30 - 2 * i)

    need, thr_u = lax.fori_loop(0, PLANE_BITS // 2, pair_body,
                                (jnp.full((tq, 1), topk, jnp.int32), jnp.zeros((tq, 1), jnp.int32)))
    thr = thr_u ^ INT_MIN
    n_eq = jnp.sum(lax.fori_loop(0, ngrp, lambda g, part: part + lax.population_count(act_ref[g]),
                                 jnp.zeros((tq, LANES), jnp.int32)), axis=1, keepdims=True)

    def count(pred):
        def body(c, part):
            hit = pred(key_ref[c], c)
            for s in range(tpc):
                part = part + hit[:, s * LANES:(s + 1) * LANES].astype(jnp.int32)
            return part
        part = lax.fori_loop(0, nkc, body, jnp.zeros((tq, LANES), jnp.int32))
        return jnp.sum(part, axis=1, keepdims=True)

    real = thr != INT_MIN
    excess = real & (n_eq > need)
    pos_bits = max(1, (ck * key_ref.shape[0] - 1).bit_length())

    def tie_search():
        def pbody(i, ans):
            cand = ans | lax.shift_left(jnp.int32(1), pos_bits - 1 - i)
            cnt = count(lambda kc, c: (kc == thr) & (c * ck + lane_pos < cand))
            return jnp.where(cnt < need, cand, ans)
        return lax.fori_loop(0, pos_bits, pbody, jnp.zeros((tq, 1), jnp.int32))

    any_excess = jnp.max(excess.astype(jnp.int32)) > 0
    tie_lim = lax.cond(any_excess, tie_search, lambda: jnp.zeros((tq, 1), jnp.int32))
    big = jnp.int32(ck * key_ref.shape[0])
    tie_lim = jnp.where(excess, tie_lim, jnp.where(real, big, -1))

    m_ref[...] = jnp.full(m_ref.shape, NEG, f32)
    acc_ref[...] = jnp.zeros(acc_ref.shape, f32)

    def qk_chunk(c, s_ref):
        for n in range(N_KV):
            s_ref[n] = _mm(qa_ref[n], kt_ref[0, c, n])

    def attn_chunk(c, s_ref, last):
        start = pl.multiple_of(c * ck, ck)
        kc = key_ref[c]
        kpos = start + lane_pos
        sel = (kc > thr) | ((kc == thr) & (kpos <= tie_lim))
        bias = jnp.where(sel, 0.0, NEG)
        if last:
            ahead = jnp.maximum(kpos - qpos, 0).astype(f32)
        for n in range(N_KV):
            ps, alphas = [], []
            for g in range(GQA_G):
                rows = slice(g * tq, (g + 1) * tq)
                sg = s_ref[n, rows, :] + bias
                if last:
                    sg = sg - (2.0 * LOG2E * _slope(n * GQA_G + g)) * ahead
                m_prev = m_ref[n, rows]
                m_new = jnp.maximum(m_prev, jnp.max(sg, axis=1, keepdims=True))
                m_ref[n, rows] = m_new
                alphas.append(jnp.exp2(m_prev - m_new))
                ps.append(jnp.exp2(sg - m_new).astype(bf16))
            pv = _mm(jnp.concatenate(ps, axis=0),
                     va_ref[0, pl.ds(start, ck), n * 2 * HEAD_DIM:(n + 1) * 2 * HEAD_DIM])
            acc_ref[n] = jnp.concatenate(alphas, axis=0) * acc_ref[n] + pv

    n_rest = nkc - 1
    qk_chunk(n_rest, sa_ref)
    qk_chunk(0, sb_ref)
    attn_chunk(n_rest, sa_ref, True)

    def attn_body(i, carry):
        qk_chunk(2 * i + 1, sa_ref)
        attn_chunk(2 * i, sb_ref, False)
        qk_chunk(2 * i + 2, sb_ref)
        attn_chunk(2 * i + 1, sa_ref, False)
        return carry

    lax.fori_loop(0, n_rest // 2, attn_body, 0)

    @pl.when(n_rest % 2 == 1)
    def _():
        attn_chunk(n_rest - 1, sb_ref, False)

    for hd in range(N_HEADS):
        n, g = divmod(hd, GQA_G)
        a = acc_ref[n, g * tq:(g + 1) * tq, :]
        o_ref[0, :, hd * HEAD_DIM:(hd + 1) * HEAD_DIM] = (a[:, :HEAD_DIM] / a[:, HEAD_DIM:]).astype(bf16)


def _sparse_attn(q, qi, wi, k_all, v_all, ki_all, pos0, tq, ck, topk):
    B, T, _ = q.shape
    S = k_all.shape[1]
    nc = -(-S // ck)
    Sp = nc * ck
    tpc = ck // LANES
    assert ck % LANES == 0 and PLANE_BITS % tpc == 0 and T % tq == 0
    cpg = PLANE_BITS // tpc
    n_grp = -(-nc // cpg)
    assert Sp <= 256 * LANES
    for qb in range(T // tq):
        p_first = pos0 + qb * tq
        limit = min(S, ((p_first + tq - 1) // CHUNK + 1) * CHUNK)
        assert p_first >= (-(-limit // ck) - 1) * ck
    pad = lambda a: jnp.pad(a.astype(bf16), ((0, 0), (0, Sp - S), (0, 0)))
    q_side, k_side = _alibi_operands(Sp)
    pos_rows = jnp.zeros((HEAD_DIM, Sp), bf16).at[:N_ALIBI].set(k_side.T)
    kt = pad(k_all).reshape(B, Sp, N_KV, HEAD_DIM).transpose(0, 2, 3, 1)
    kt = jnp.concatenate([kt, jnp.broadcast_to(pos_rows, kt.shape)], axis=2)
    kt = kt.reshape(B, N_KV, 2 * HEAD_DIM, nc, ck).transpose(0, 3, 1, 2, 4)
    va = pad(v_all).reshape(B, Sp, N_KV, HEAD_DIM)
    va = jnp.concatenate([va, jnp.ones_like(va)], axis=-1).reshape(B, Sp, N_KV * 2 * HEAD_DIM)
    kit = pad(ki_all).transpose(0, 2, 1).reshape(B, IDX_DIM, nc, ck).transpose(0, 2, 1, 3)
    qs = jnp.zeros((N_HEADS, tq, HEAD_DIM), bf16).at[:, :, :N_ALIBI].set(q_side[:, None, :])

    kern = functools.partial(_sparse_attn_kernel, tq=tq, ck=ck, S=S, pos0=pos0, topk=topk)
    hq = N_HEADS * HEAD_DIM
    once = pl.Buffered(1)
    return pl.pallas_call(
        kern,
        grid=(B, T // tq),
        in_specs=[pl.BlockSpec((1, tq, hq), lambda b, t: (b, t, 0)),
                  pl.BlockSpec((N_HEADS, tq, HEAD_DIM), lambda b, t: (0, 0, 0), pipeline_mode=once),
                  pl.BlockSpec((1, tq, IDX_HEADS * IDX_DIM), lambda b, t: (b, t, 0)),
                  pl.BlockSpec((1, tq, IDX_HEADS), lambda b, t: (b, t, 0)),
                  pl.BlockSpec((1, nc, N_KV, 2 * HEAD_DIM, ck), lambda b, t: (b, 0, 0, 0, 0), pipeline_mode=once),
                  pl.BlockSpec((1, Sp, N_KV * 2 * HEAD_DIM), lambda b, t: (b, 0, 0), pipeline_mode=once),
                  pl.BlockSpec((1, nc, IDX_DIM, ck), lambda b, t: (b, 0, 0, 0), pipeline_mode=once)],
        out_specs=pl.BlockSpec((1, tq, hq), lambda b, t: (b, t, 0)),
        out_shape=jax.ShapeDtypeStruct((B, T, hq), bf16),
        scratch_shapes=[pltpu.VMEM((n_grp * cpg, tq, ck), jnp.int32),
                        pltpu.VMEM((n_grp, PLANE_BITS, tq, LANES), jnp.int32),
                        pltpu.VMEM((n_grp, tq, LANES), jnp.int32),
                        pltpu.VMEM((N_KV, GQA_G * tq, 2 * HEAD_DIM), bf16),
                        pltpu.VMEM((N_KV, GQA_G * tq, ck), f32),
                        pltpu.VMEM((N_KV, GQA_G * tq, ck), f32),
                        pltpu.VMEM((N_KV, GQA_G * tq, 1), f32),
                        pltpu.VMEM((N_KV, GQA_G * tq, 2 * HEAD_DIM), f32)],
        compiler_params=_cparams(2),
    )(q, qs, qi, wi, kt, va, kit)


SPAN = PLANE_BITS * SUBLANES
ATTN_TQ = 256
ONES_ROWS = 16


def _sparse_attn_t_kernel(qt_ref, qst_ref, qit_ref, wit_ref, k_ref, pos_ref, vt_ref, ki_ref, ot_ref,
                          key_ref, plane_ref, act_ref, qa_ref, sa_ref, sb_ref, mxa_ref, mxb_ref, m_ref, acc_ref,
                          *, tq, ck, S, pos0, topk):
    spc = ck // SPAN
    wr = spc * SUBLANES
    ltiles = tq // LANES
    qb = pl.program_id(1)
    p_first = pos0 + qb * tq
    limit = jnp.minimum(S, ((p_first + tq - 1) // CHUNK + 1) * CHUNK)
    nkc = (limit + ck - 1) // ck
    qpos = p_first + lax.broadcasted_iota(jnp.int32, (1, tq), 1)
    qchunk = qpos // CHUNK
    row_pos = lax.broadcasted_iota(jnp.int32, (ck, tq), 0)
    wit = wit_ref[0]

    for hd in range(N_HEADS):
        n, g = divmod(hd, GQA_G)
        qa_ref[n, 0:HEAD_DIM, g * tq:(g + 1) * tq] = qt_ref[0, hd * HEAD_DIM:(hd + 1) * HEAD_DIM, :]
        qa_ref[n, HEAD_DIM:2 * HEAD_DIM, g * tq:(g + 1) * tq] = qst_ref[hd]

    n_rest = nkc - 1

    def score_chunk(c, last):
        kic = ki_ref[0, pl.ds(pl.multiple_of(c * ck, ck), ck), :]
        sc = jnp.zeros((ck, tq), f32)
        for hd in range(IDX_HEADS):
            d = _mm(kic, qit_ref[0, hd * IDX_DIM:(hd + 1) * IDX_DIM, :])
            sc = sc + wit[hd:hd + 1, :] * jnp.maximum(d, 0.0)
        bits = pltpu.bitcast(sc, jnp.int32)
        key = bits ^ ((bits >> 31) & 0x7FFFFFFF)
        if last:
            kpos = c * ck + row_pos
            key = jnp.where((kpos // CHUNK <= qchunk) & (kpos < S), key, INT_MIN)
        key_ref[c] = key
        for u in range(spc):
            for lt in range(ltiles):
                w = [key[u * SPAN + k * SUBLANES:u * SPAN + (k + 1) * SUBLANES, lt * LANES:(lt + 1) * LANES]
                     for k in range(PLANE_BITS)]
                for j, msk in BUTTERFLY:
                    for k in range(PLANE_BITS):
                        if k & j == 0:
                            tmp = (w[k] ^ lax.shift_right_logical(w[k + j], jnp.int32(j))) & msk
                            w[k] = w[k] ^ tmp
                            w[k + j] = w[k + j] ^ (tmp << j)
                rows = pl.ds(pl.multiple_of(c * wr + u * SUBLANES, SUBLANES), SUBLANES)
                lanes = slice(lt * LANES, (lt + 1) * LANES)
                plane_ref[0, rows, lanes] = jnp.full((SUBLANES, LANES), -1, jnp.int32)
                plane_ref[1, rows, lanes] = jnp.full((SUBLANES, LANES), -1, jnp.int32)
                plane_ref[2, rows, lanes] = ~w[0]
                for i in range(1, PLANE_BITS):
                    plane_ref[i + 2, rows, lanes] = w[i]
        act_ref[pl.ds(pl.multiple_of(c * wr, wr), wr), :] = jnp.full((wr, tq), -1, jnp.int32)

    def score_body(c, carry):
        score_chunk(c, False)
        return carry

    lax.fori_loop(0, n_rest, score_body, 0)
    score_chunk(n_rest, True)

    def chunk_rows(c):
        return pl.ds(pl.multiple_of(c * wr, wr), wr)

    def narrowed(c, i, flip1, flip2):
        rows = chunk_rows(c)
        return act_ref[rows, :] & (plane_ref[2 * i, rows, :] ^ flip1) & (plane_ref[2 * i + 1, rows, :] ^ flip2)

    def pair_body(i, carry):
        need, thr_u, flip1, flip2 = carry

        def csum(c, parts):
            rows = chunk_rows(c)
            a = narrowed(c, i, flip1, flip2)
            act_ref[rows, :] = a
            p1, p2 = plane_ref[2 * i + 2, rows, :], plane_ref[2 * i + 3, rows, :]
            a1 = a & p1
            a11 = a1 & p2
            return (parts[0] + lax.population_count(a11), parts[1] + lax.population_count(a1 ^ a11),
                    parts[2] + lax.population_count((a ^ a1) & p2))

        zero = jnp.zeros((wr, tq), jnp.int32)
        parts = lax.fori_loop(0, nkc, csum, (zero, zero, zero))
        t1 = jnp.sum(parts[0], axis=0, keepdims=True)
        t2 = t1 + jnp.sum(parts[1], axis=0, keepdims=True)
        t3 = t2 + jnp.sum(parts[2], axis=0, keepdims=True)
        hi = need <= t2
        lo = (need <= t1) | (jnp.logical_not(hi) & (need <= t3))
        need = need - jnp.where(hi, jnp.where(lo, 0, t1), jnp.where(lo, t2, t3))
        bits = jnp.where(hi, 2, 0) | jnp.where(lo, 1, 0)
        return need, thr_u | lax.shift_left(bits, 30 - 2 * i), jnp.where(hi, 0, -1), jnp.where(lo, 0, -1)

    zero_q = jnp.zeros((1, tq), jnp.int32)
    need, thr_u, flip1, flip2 = lax.fori_loop(0, PLANE_BITS // 2, pair_body,
                                              (jnp.full((1, tq), topk, jnp.int32), zero_q, zero_q, zero_q))
    thr = thr_u ^ INT_MIN
    n_eq = jnp.sum(lax.fori_loop(
        0, nkc, lambda c, part: part + lax.population_count(narrowed(c, PLANE_BITS // 2, flip1, flip2)),
        jnp.zeros((wr, tq), jnp.int32)), axis=0, keepdims=True)

    def count(pred):
        def body(c, part):
            return part + jnp.sum(pred(key_ref[c], c).astype(jnp.int32), axis=0, keepdims=True)
        return lax.fori_loop(0, nkc, body, jnp.zeros((1, tq), jnp.int32))

    real = thr != INT_MIN
    excess = real & (n_eq > need)
    pos_bits = max(1, (ck * key_ref.shape[0] - 1).bit_length())

    def tie_search():
        def pbody(i, ans):
            cand = ans | lax.shift_left(jnp.int32(1), pos_bits - 1 - i)
            cnt = count(lambda kc, c: (kc == thr) & (c * ck + row_pos < cand))
            return jnp.where(cnt < need, cand, ans)
        return lax.fori_loop(0, pos_bits, pbody, jnp.zeros((1, tq), jnp.int32))

    any_excess = jnp.max(excess.astype(jnp.int32)) > 0
    tie_lim = lax.cond(any_excess, tie_search, lambda: jnp.zeros((1, tq), jnp.int32))
    big = jnp.int32(ck * key_ref.shape[0])
    tie_lim = jnp.where(excess, tie_lim, jnp.where(real, big, -1))

    def qk_chunk(c, s_ref, mx_ref, first):
        rows = pl.ds(pl.multiple_of(c * ck, ck), ck)
        kc = key_ref[c]
        kpos = c * ck + row_pos
        sel = (kc > thr) | ((kc == thr) & (kpos <= tie_lim))
        bias = jnp.where(sel, 0.0, NEG)
        if first:
            ahead = jnp.maximum(kpos - qpos, 0).astype(f32)
        for n in range(N_KV):
            ka = jnp.concatenate([k_ref[0, rows, n * HEAD_DIM:(n + 1) * HEAD_DIM], pos_ref[rows, :]], axis=1)
            s = _mm(ka, qa_ref[n])
            for g in range(GQA_G):
                cols = slice(g * tq, (g + 1) * tq)
                sg = s[:, cols] + bias
                if first:
                    sg = sg - (2.0 * LOG2E * _slope(n * GQA_G + g)) * ahead
                s_ref[n, :, cols] = sg
                mx_ref[n, :, cols] = jnp.max(sg, axis=0, keepdims=True)

    def attn_chunk(c, s_ref, mx_ref, first):
        for n in range(N_KV):
            ps, alphas = [], []
            for g in range(GQA_G):
                cols = slice(g * tq, (g + 1) * tq)
                if first:
                    m_new = mx_ref[n, :, cols]
                else:
                    m_prev = m_ref[n, :, cols]
                    m_new = jnp.maximum(m_prev, mx_ref[n, :, cols])
                    alphas.append(jnp.exp2(m_prev - m_new))
                m_ref[n, :, cols] = m_new
                ps.append(jnp.exp2(s_ref[n, :, cols] - m_new).astype(bf16))
            vat = jnp.concatenate([vt_ref[0, c, n], jnp.ones((ONES_ROWS, ck), bf16)], axis=0)
            pv = _mm(vat, jnp.concatenate(ps, axis=1))
            acc_ref[n] = pv if first else jnp.concatenate(alphas, axis=1) * acc_ref[n] + pv

    qk_chunk(n_rest, sa_ref, mxa_ref, True)
    qk_chunk(0, sb_ref, mxb_ref, False)
    attn_chunk(n_rest, sa_ref, mxa_ref, True)

    def attn_body(i, carry):
        qk_chunk(2 * i + 1, sa_ref, mxa_ref, False)
        attn_chunk(2 * i, sb_ref, mxb_ref, False)
        qk_chunk(2 * i + 2, sb_ref, mxb_ref, False)
        attn_chunk(2 * i + 1, sa_ref, mxa_ref, False)
        return carry

    lax.fori_loop(0, n_rest // 2, attn_body, 0)

    @pl.when(n_rest % 2 == 1)
    def _():
        attn_chunk(n_rest - 1, sb_ref, mxb_ref, False)

    for hd in range(N_HEADS):
        n, g = divmod(hd, GQA_G)
        a = acc_ref[n, :, g * tq:(g + 1) * tq]
        ot_ref[0, hd * HEAD_DIM:(hd + 1) * HEAD_DIM, :] = (a[:HEAD_DIM] / a[HEAD_DIM:HEAD_DIM + 1]).astype(bf16)


def _key_operands_t(k_all, v_all, ki_all, ck):
    B, S, _ = k_all.shape
    nc = -(-S // ck)
    pad = lambda a: jnp.pad(a.astype(bf16), ((0, 0), (0, nc * ck - S), (0, 0)))
    vt = pad(v_all).reshape(B, nc, ck, N_KV, HEAD_DIM).transpose(0, 1, 3, 4, 2)
    return pad(k_all), vt, pad(ki_all)


def _sparse_attn_t(qt, qit, wit, kb, vt, kib, S, pos0, tq, ck, topk):
    B, _, T = qt.shape
    nc = vt.shape[1]
    Sp = nc * ck
    assert kb.shape[1] == Sp and kib.shape[1] == Sp and Sp - ck < S <= Sp
    assert ck % SPAN == 0 and tq % LANES == 0 and T % tq == 0
    assert Sp <= 256 * LANES
    for qb in range(T // tq):
        p_first = pos0 + qb * tq
        limit = min(S, ((p_first + tq - 1) // CHUNK + 1) * CHUNK)
        assert p_first >= (-(-limit // ck) - 1) * ck
    q_side, k_side = _alibi_operands(Sp)
    pos_cols = jnp.zeros((Sp, HEAD_DIM), bf16).at[:, :N_ALIBI].set(k_side)
    qst = jnp.zeros((N_HEADS, HEAD_DIM, tq), bf16).at[:, :N_ALIBI, :].set(q_side[:, :, None])

    kern = functools.partial(_sparse_attn_t_kernel, tq=tq, ck=ck, S=S, pos0=pos0, topk=topk)
    hq = N_HEADS * HEAD_DIM
    once = pl.Buffered(1)
    wr = ck // SPAN * SUBLANES
    return pl.pallas_call(
        kern,
        grid=(B, T // tq),
        in_specs=[pl.BlockSpec((1, hq, tq), lambda b, t: (b, 0, t)),
                  pl.BlockSpec((N_HEADS, HEAD_DIM, tq), lambda b, t: (0, 0, 0), pipeline_mode=once),
                  pl.BlockSpec((1, IDX_HEADS * IDX_DIM, tq), lambda b, t: (b, 0, t)),
                  pl.BlockSpec((1, IDX_HEADS, tq), lambda b, t: (b, 0, t)),
                  pl.BlockSpec((1, Sp, N_KV * HEAD_DIM), lambda b, t: (b, 0, 0), pipeline_mode=once),
                  pl.BlockSpec((Sp, HEAD_DIM), lambda b, t: (0, 0), pipeline_mode=once),
                  pl.BlockSpec((1, nc, N_KV, HEAD_DIM, ck), lambda b, t: (b, 0, 0, 0, 0), pipeline_mode=once),
                  pl.BlockSpec((1, Sp, IDX_DIM), lambda b, t: (b, 0, 0), pipeline_mode=once)],
        out_specs=pl.BlockSpec((1, hq, tq), lambda b, t: (b, 0, t)),
        out_shape=jax.ShapeDtypeStruct((B, hq, T), bf16),
        scratch_shapes=[pltpu.VMEM((nc, ck, tq), jnp.int32),
                        pltpu.VMEM((PLANE_BITS + 2, nc * wr, tq), jnp.int32),
                        pltpu.VMEM((nc * wr, tq), jnp.int32),
                        pltpu.VMEM((N_KV, 2 * HEAD_DIM, GQA_G * tq), bf16),
                        pltpu.VMEM((N_KV, ck, GQA_G * tq), f32),
                        pltpu.VMEM((N_KV, ck, GQA_G * tq), f32),
                        pltpu.VMEM((N_KV, 1, GQA_G * tq), f32),
                        pltpu.VMEM((N_KV, 1, GQA_G * tq), f32),
                        pltpu.VMEM((N_KV, 1, GQA_G * tq), f32),
                        pltpu.VMEM((N_KV, HEAD_DIM + ONES_ROWS, GQA_G * tq), f32)],
        compiler_params=_cparams(2),
    )(qt, qst, qit, wit, kb, pos_cols, vt, kib)


def _merge_kernel(x_ref, h_ref, ya_ref, yb_ref, o_ref, g1_ref, wgl_ref, wpc_ref, wo_ref,
                  gn2_ref, sc2_ref, sh2_ref, x1_ref, h2_ref, *, bb, tt, o_on_lanes):
    D = D_MODEL
    M = bb * tt
    h = h_ref[...].reshape(M, D)
    if o_on_lanes:
        yc = lax.dot_general(o_ref[0], wpc_ref[...], (((0,), (0,)), ((), ())), preferred_element_type=f32)
    else:
        yc = _mm(o_ref[...].reshape(M, D), wpc_ref[...])
    m = _sigmoid(_mm(h, wgl_ref[:, 0:D])) * ya_ref[...].reshape(M, D).astype(f32)
    m = m + _sigmoid(_mm(h, wgl_ref[:, D:2 * D])) * yb_ref[...].reshape(M, D).astype(f32)
    m = m + _sigmoid(_mm(h, wgl_ref[:, 2 * D:3 * D])) * yc
    x1 = x_ref[...] + g1_ref[...] * _mm(m.astype(bf16), wo_ref[...]).reshape(bb, tt, D)
    x1_ref[...] = x1
    h2 = _rms(x1, gn2_ref[...]) * (1.0 + sc2_ref[...]) + sh2_ref[...]
    h2_ref[...] = h2.astype(bf16)


def _merge(x, h, ya, yb, o, g1, wgl, wpc, wo, gn2, sc2, sh2, o_on_lanes):
    B, T, D = x.shape
    bb, tt = _tiles(B, T)
    assert bb == 1 or not o_on_lanes
    kern = functools.partial(_merge_kernel, bb=bb, tt=tt, o_on_lanes=o_on_lanes)
    row = _row_spec(bb, tt, D)
    vec = _batch_spec(bb, 1, D)
    o_spec = pl.BlockSpec((1, D, tt), lambda b, t: (b, 0, t)) if o_on_lanes else row
    return pl.pallas_call(
        kern,
        grid=(B // bb, T // tt),
        in_specs=[row, row, row, row, o_spec, vec, _const_spec((D, 3 * D)), _const_spec((D, D)),
                  _const_spec((D, D)), _const_spec((1, 1, D)), vec, vec],
        out_specs=[row, row],
        out_shape=[jax.ShapeDtypeStruct((B, T, D), f32), jax.ShapeDtypeStruct((B, T, D), bf16)],
        compiler_params=_cparams(2),
    )(x, h, ya, yb, o, g1, wgl, wpc, wo, gn2.reshape(1, 1, D), sc2, sh2)


FF_TILE = 1024


def _mlp_kernel(x1_ref, h2_ref, g2_ref, w1_ref, b1_ref, w2_ref, b2_ref, x2_ref, *, bb, tt):
    D = D_MODEL
    M = bb * tt
    h2 = h2_ref[...].reshape(M, D)
    acc = jnp.zeros((M, D), f32)
    for c0 in range(0, D_FF, FF_TILE):
        f = jnp.maximum(_mm(h2, w1_ref[:, c0:c0 + FF_TILE]) + b1_ref[:, c0:c0 + FF_TILE], 0.0)
        acc = acc + _mm((f * f).astype(bf16), w2_ref[c0:c0 + FF_TILE, :])
    f = (acc + b2_ref[...]).reshape(bb, tt, D)
    x2_ref[...] = x1_ref[...] + g2_ref[...] * f


def _mlp(x1, h2, g2, w1, b1, w2, b2):
    B, T, D = x1.shape
    bb, tt = _tiles(B, T)
    kern = functools.partial(_mlp_kernel, bb=bb, tt=tt)
    row = _row_spec(bb, tt, D)
    return pl.pallas_call(
        kern,
        grid=(B // bb, T // tt),
        in_specs=[row, row, _batch_spec(bb, 1, D), _const_spec((D, D_FF)), _const_spec((1, D_FF)),
                  _const_spec((D_FF, D)), _const_spec((1, D))],
        out_specs=row,
        out_shape=jax.ShapeDtypeStruct((B, T, D), f32),
        compiler_params=_cparams(2),
    )(x1, h2, g2, w1, b1.reshape(1, D_FF), w2, b2.reshape(1, D))


def _split_w_in(w_in):
    D = D_MODEL
    w = w_in.astype(bf16)
    o_b = 2 * D
    o_q = 4 * D
    o_gl = o_q + N_HEADS * HEAD_DIM + 2 * N_KV * HEAD_DIM + IDX_HEADS * IDX_DIM + IDX_DIM + IDX_HEADS
    n_c = o_gl - o_q
    w_c = jnp.pad(w[:, o_q:o_gl], ((0, 0), (0, QKV_COLS - n_c)))
    return dict(wa1=w[:, 0:D], wa2=w[:, D:2 * D], wbu=w[:, o_b:o_b + D], wbv=w[:, o_b + D:o_b + 2 * D],
                w_c=w_c, wgl=w[:, o_gl:o_gl + 3 * D])


def _layer(x, mod, cache, W, pos0):
    B, T, D = x.shape
    sh1, sc1, g1, sh2, sc2, g2 = mod
    win = W['w_in']
    h = _prenorm(x, W['g_norm1'], sc1, sh1)

    hist = jnp.zeros((B, HIST, D), f32) if cache is None else cache[3]
    ya, new_conv = _branch_conv(h, hist, win['wa1'], win['wa2'], W['w_dw'], W['b_dw'],
                                W['ln_c_g'], W['ln_c_b'], W['w_pa'])

    L = GMLP_CHUNK if cache is None else T
    res = _branch_gmlp(h, win['wbu'], win['wbv'], W['ln_v_g'], W['ln_v_b'], W['w_s'], W['b_s'],
                       W['w_pb'], L, cache is not None)
    yb = res[0]
    vg = res[1] if cache is not None else None

    on_lanes = T % ATTN_TQ == 0
    q, k, v, qi, ki, wi, *key_ops = _attn_proj(h, win['w_c'], W['g_q'], W['g_k'], on_lanes)
    if cache is None:
        k_all, v_all, ki_all = k, v, ki
    else:
        k_all = jnp.concatenate([cache[0].reshape(B, -1, N_KV * HEAD_DIM), k], axis=1)
        v_all = jnp.concatenate([cache[1].reshape(B, -1, N_KV * HEAD_DIM), v], axis=1)
        ki_all = jnp.concatenate([cache[2], ki], axis=1)
    S = k_all.shape[1]
    topk = min(TOPK_MAX, S // 4)
    if on_lanes:
        ck = ROW_TILE
        if cache is not None:
            key_ops = _key_operands_t(k_all, v_all, ki_all, ck)
        o = _sparse_attn_t(q, qi, wi, *key_ops, S, pos0, ATTN_TQ, ck, topk)
    else:
        o = _sparse_attn(q, qi, wi, k_all, v_all, ki_all, pos0, min(Q_BLOCK, T), 2 * LANES, topk)

    x1, h2 = _merge(x, h, ya, yb, o, g1, win['wgl'], W['w_pc'], W['w_o'], W['g_norm2'], sc2, sh2, on_lanes)
    x2 = _mlp(x1, h2, g2, W['w_1'], W['b_1'], W['w_2'], W['b_2'])
    return x2, (k.reshape(B, T, N_KV, HEAD_DIM), v.reshape(B, T, N_KV, HEAD_DIM), ki, new_conv, vg)


def kernel(x_prompt, x_sample, c_prompt, c_sample, cache_k, cache_v, cache_kidx, cache_conv, w_ada, b_ada, g_norm1, w_in, w_dw, b_dw, ln_c_g, ln_c_b, w_pa, ln_v_g, ln_v_b, w_s, b_s, w_pb, g_q, g_k, w_pc, w_o, g_norm2, w_1, b_1, w_2, b_2):
    depth = w_in.shape[0]
    D = D_MODEL
    Bp, Bs = x_prompt.shape[0], x_sample.shape[0]
    past = cache_k.shape[2]
    c_all = jnp.concatenate([c_prompt, c_sample], axis=0)
    n_c = c_all.shape[0]
    n_pad = -(-n_c // SUBLANES) * SUBLANES
    mod_all = _ada(jnp.pad(c_all, ((0, n_pad - n_c), (0, 0))), w_ada.astype(bf16), b_ada)

    def mods(l, lo, n):
        m = mod_all[l, lo:lo + n]
        return [m[:, None, i * D:(i + 1) * D] for i in range(6)]

    hp, hs = x_prompt, x_sample
    outs_p, outs_s = [], []
    for l in range(depth):
        W = dict(g_norm1=g_norm1[l], w_in=_split_w_in(w_in[l]), w_dw=w_dw[l], b_dw=b_dw[l],
                 ln_c_g=ln_c_g[l], ln_c_b=ln_c_b[l], w_pa=w_pa[l].astype(bf16), ln_v_g=ln_v_g[l],
                 ln_v_b=ln_v_b[l], w_s=w_s[l], b_s=b_s[l], w_pb=w_pb[l].astype(bf16), g_q=g_q[l],
                 g_k=g_k[l], w_pc=w_pc[l].astype(bf16), w_o=w_o[l].astype(bf16), g_norm2=g_norm2[l],
                 w_1=w_1[l].astype(bf16), b_1=b_1[l], w_2=w_2[l].astype(bf16), b_2=b_2[l])
        hp, op = _layer(hp, mods(l, 0, Bp), None, W, 0)
        hs, os_ = _layer(hs, mods(l, Bp, Bs), (cache_k[l], cache_v[l], cache_kidx[l], cache_conv[l]), W, past)
        outs_p.append(op)
        outs_s.append(os_)
    stack = lambda outs, i: jnp.stack([o[i] for o in outs])
    return (hp, hs,
            stack(outs_p, 0), stack(outs_p, 1), stack(outs_p, 2), stack(outs_p, 3),
            stack(outs_s, 0), stack(outs_s, 1), stack(outs_s, 2), stack(outs_s, 3), stack(outs_s, 4))
```

```python
import functools

import jax
import jax.numpy as jnp
from jax import lax
from jax.experimental import pallas as pl
from jax.experimental.pallas import tpu as pltpu

D_MODEL = 1024
CHUNK = 64
CONV_W = 31
HIST = CONV_W - 1
GMLP_CHUNK = 128
GMLP_GROUPS = 8
GMLP_GW = D_MODEL // GMLP_GROUPS
N_HEADS = 8
HEAD_DIM = 128
N_KV = 2
GQA_G = N_HEADS // N_KV
IDX_HEADS = 8
IDX_DIM = 64
TOPK_MAX = 256
Q_BLOCK = 128
D_FF = 4 * D_MODEL
EPS = 1e-6

LANES = 128
SUBLANES = 8
HALO = 32
ROW_TILE = 512
VMEM_LIMIT = 56 * 1024 * 1024
INT_MIN = -2 ** 31
NEG = -1e30

f32 = jnp.float32
bf16 = jnp.bfloat16


def _cparams(n_axes):
    return pltpu.CompilerParams(dimension_semantics=("arbitrary",) * n_axes,
                                vmem_limit_bytes=VMEM_LIMIT)


def _tiles(B, T):
    if T >= ROW_TILE:
        assert T % ROW_TILE == 0
        return 1, ROW_TILE
    assert ROW_TILE % T == 0 and B % (ROW_TILE // T) == 0
    return ROW_TILE // T, T


def _row_spec(bb, tt, n):
    return pl.BlockSpec((bb, tt, n), lambda b, t: (b, t, 0))


def _batch_spec(bb, r, n):
    return pl.BlockSpec((bb, r, n), lambda b, t: (b, 0, 0))


def _const_spec(shape):
    nd = len(shape)
    return pl.BlockSpec(shape, lambda b, t: (0,) * nd, pipeline_mode=pl.Buffered(1))


def _mm(a, w):
    return jnp.dot(a, w, preferred_element_type=f32)


def _sigmoid(x):
    return 1.0 / (1.0 + jnp.exp(-x))


def _silu(x):
    return x * _sigmoid(x)


def _gelu_tanh(x):
    return 0.5 * x * (1.0 + jnp.tanh(0.7978845608028654 * (x + 0.044715 * (x * x * x))))


def _rms(x, g):
    return x * lax.rsqrt(jnp.mean(x * x, axis=-1, keepdims=True) + EPS) * g


def _ln(x, g, b):
    mu = jnp.mean(x, axis=-1, keepdims=True)
    xc = x - mu
    var = jnp.mean(xc * xc, axis=-1, keepdims=True)
    return xc * lax.rsqrt(var + EPS) * g + b


def _ada_kernel(c_ref, w_ref, b_ref, o_ref):
    c = c_ref[...]
    o_ref[0] = _mm(_silu(c).astype(bf16), w_ref[0]) + b_ref[0]


def _ada(c_all, w_ada, b_ada):
    depth, d, n = w_ada.shape
    bp = c_all.shape[0]
    tn = 1024
    return pl.pallas_call(
        _ada_kernel,
        grid=(depth, n // tn),
        in_specs=[pl.BlockSpec((bp, d), lambda l, j: (0, 0)),
                  pl.BlockSpec((1, d, tn), lambda l, j: (l, 0, j)),
                  pl.BlockSpec((1, 1, tn), lambda l, j: (l, 0, j))],
        out_specs=pl.BlockSpec((1, bp, tn), lambda l, j: (l, 0, j)),
        out_shape=jax.ShapeDtypeStruct((depth, bp, n), f32),
        compiler_params=_cparams(2),
    )(c_all, w_ada, b_ada.reshape(depth, 1, n))


def _prenorm_kernel(x_ref, g_ref, sc_ref, sh_ref, h_ref):
    h = _rms(x_ref[...], g_ref[...]) * (1.0 + sc_ref[...]) + sh_ref[...]
    h_ref[...] = h.astype(bf16)


def _prenorm(x, g, sc, sh):
    B, T, D = x.shape
    bb, tt = _tiles(B, T)
    return pl.pallas_call(
        _prenorm_kernel,
        grid=(B // bb, T // tt),
        in_specs=[_row_spec(bb, tt, D), _const_spec((1, 1, D)),
                  _batch_spec(bb, 1, D), _batch_spec(bb, 1, D)],
        out_specs=_row_spec(bb, tt, D),
        out_shape=jax.ShapeDtypeStruct((B, T, D), bf16),
        compiler_params=_cparams(2),
    )(x, g.reshape(1, 1, D), sc, sh)


CONV_ROWS = 64
CONV_COLS = 256


def _conv_kernel(h_ref, hist_ref, wa1_ref, wa2_ref, wdw_ref, bdw_ref, lng_ref, lnb_ref, wpa_ref,
                 ya_ref, newc_ref, apad_ref, conv_ref, shift_ref, *, bb, tt):
    D = D_MODEL
    t = pl.program_id(1)

    @pl.when(t == 0)
    def _():
        apad_ref[:, 0:HALO - HIST, :] = jnp.zeros((bb, HALO - HIST, D), f32)
        apad_ref[:, HALO - HIST:HALO, :] = hist_ref[...]

    h = h_ref[...].reshape(bb * tt, D)
    a = _mm(h, wa1_ref[...]) * _sigmoid(_mm(h, wa2_ref[...]))
    apad_ref[:, HALO:HALO + tt, :] = a.reshape(bb, tt, D)

    base = HALO - HIST
    n_sh = HALO + tt - SUBLANES
    for b in range(bb):
        for c0 in range(0, D, CONV_COLS):
            cols = slice(c0, c0 + CONV_COLS)
            for s in range(1, SUBLANES):
                shift_ref[s - 1, 0:n_sh, :] = apad_ref[b, s:s + n_sh, cols]
            for r0 in range(0, tt, CONV_ROWS):
                acc = jnp.zeros((CONV_ROWS, CONV_COLS), f32)
                for j in range(CONV_W):
                    q8, s = divmod(base + j, SUBLANES)
                    lo = r0 + SUBLANES * q8
                    src = (apad_ref[b, lo:lo + CONV_ROWS, cols] if s == 0
                           else shift_ref[s - 1, lo:lo + CONV_ROWS, :])
                    acc = acc + wdw_ref[j:j + 1, cols] * src
                conv_ref[b * tt + r0:b * tt + r0 + CONV_ROWS, cols] = acc

    tail = apad_ref[:, HALO + tt - HIST:HALO + tt, :]
    newc_ref[...] = tail
    apad_ref[:, HALO - HIST:HALO, :] = tail

    y = conv_ref[...] + bdw_ref[...]
    z = _silu(_ln(y, lng_ref[...], lnb_ref[...]))
    ya_ref[...] = _mm(z.astype(bf16), wpa_ref[...]).astype(bf16).reshape(bb, tt, D)


def _branch_conv(h, hist, wa1, wa2, wdw, bdw, lng, lnb, wpa):
    B, T, D = h.shape
    bb, tt = _tiles(B, T)
    kern = functools.partial(_conv_kernel, bb=bb, tt=tt)
    return pl.pallas_call(
        kern,
        grid=(B // bb, T // tt),
        in_specs=[_row_spec(bb, tt, D), _batch_spec(bb, HIST, D),
                  _const_spec((D, D)), _const_spec((D, D)), _const_spec((CONV_W, D)),
                  _const_spec((1, D)), _const_spec((1, D)), _const_spec((1, D)), _const_spec((D, D))],
        out_specs=[_row_spec(bb, tt, D), _batch_spec(bb, HIST, D)],
        out_shape=[jax.ShapeDtypeStruct((B, T, D), bf16), jax.ShapeDtypeStruct((B, HIST, D), f32)],
        scratch_shapes=[pltpu.VMEM((bb, HALO + tt, D), f32), pltpu.VMEM((bb * tt, D), f32),
                        pltpu.VMEM((SUBLANES - 1, HALO + tt, CONV_COLS), f32)],
        compiler_params=_cparams(2),
    )(h, hist, wa1, wa2, wdw, bdw.reshape(1, D), lng.reshape(1, D), lnb.reshape(1, D), wpa)


def _gmlp_kernel(h_ref, wbu_ref, wbv_ref, lng_ref, lnb_ref, wm_ref, bs_ref, wpb_ref,
                 yb_ref, *rest, bb, tt, L, emit_v):
    D = D_MODEL
    if emit_v:
        vg_ref, s_ref = rest
    else:
        (s_ref,) = rest
    M = bb * tt
    h = h_ref[...].reshape(M, D)
    u = _gelu_tanh(_mm(h, wbu_ref[...]))
    vg = _ln(_gelu_tanh(_mm(h, wbv_ref[...])), lng_ref[...], lnb_ref[...])
    if emit_v:
        vg_ref[...] = vg.reshape(bb, tt, D)
    vgb = vg.astype(bf16)
    for c in range(M // L):
        for g in range(GMLP_GROUPS):
            blk = vgb[c * L:(c + 1) * L, g * GMLP_GW:(g + 1) * GMLP_GW]
            s_ref[c * L:(c + 1) * L, g * GMLP_GW:(g + 1) * GMLP_GW] = _mm(wm_ref[g], blk) + bs_ref[:, g * GMLP_GW:(g + 1) * GMLP_GW]
    yb_ref[...] = _mm((u * s_ref[...]).astype(bf16), wpb_ref[...]).astype(bf16).reshape(bb, tt, D)


def _branch_gmlp(h, wbu, wbv, lng, lnb, w_s, b_s, wpb, L, emit_v):
    B, T, D = h.shape
    bb, tt = _tiles(B, T)
    assert tt % L == 0
    tril = jnp.tril(jnp.ones((L, L), dtype=bool))
    wm = jnp.where(tril[None], w_s[:, :L, :L], 0).astype(bf16)
    bs = jnp.repeat(jnp.swapaxes(b_s[:, :L], 0, 1), GMLP_GW, axis=1)
    kern = functools.partial(_gmlp_kernel, bb=bb, tt=tt, L=L, emit_v=emit_v)
    out_specs = [_row_spec(bb, tt, D)]
    out_shape = [jax.ShapeDtypeStruct((B, T, D), bf16)]
    if emit_v:
        out_specs.append(_row_spec(bb, tt, D))
        out_shape.append(jax.ShapeDtypeStruct((B, T, D), f32))
    return pl.pallas_call(
        kern,
        grid=(B // bb, T // tt),
        in_specs=[_row_spec(bb, tt, D), _const_spec((D, D)), _const_spec((D, D)),
                  _const_spec((1, D)), _const_spec((1, D)), _const_spec((GMLP_GROUPS, L, L)),
                  _const_spec((L, D)), _const_spec((D, D))],
        out_specs=out_specs,
        out_shape=out_shape,
        scratch_shapes=[pltpu.VMEM((bb * tt, D), f32)],
        compiler_params=_cparams(2),
    )(h, wbu, wbv, lng.reshape(1, D), lnb.reshape(1, D), wm, bs, wpb)


QKV_COLS = N_HEADS * HEAD_DIM + 2 * N_KV * HEAD_DIM + IDX_HEADS * IDX_DIM + LANES


def _attn_proj_kernel(h_ref, w_ref, gq_ref, gk_ref, q_ref, k_ref, v_ref, qi_ref, ki_ref, wi_ref, *rest,
                      bb, tt, queries_on_lanes):
    D = D_MODEL
    M = bb * tt
    p = _mm(h_ref[...].reshape(M, D), w_ref[...])
    if queries_on_lanes:
        kb_ref, vt_ref, kib_ref = rest
    o = 0
    for hd in range(N_HEADS):
        qh = _rms(p[:, o:o + HEAD_DIM], gq_ref[...]) * (HEAD_DIM ** -0.5 * LOG2E)
        if queries_on_lanes:
            q_ref[0, hd * HEAD_DIM:(hd + 1) * HEAD_DIM, :] = qh.T.astype(bf16)
        else:
            q_ref[:, :, hd * HEAD_DIM:(hd + 1) * HEAD_DIM] = qh.astype(bf16).reshape(bb, tt, HEAD_DIM)
        o += HEAD_DIM
    for n in range(N_KV):
        kn = _rms(p[:, o:o + HEAD_DIM], gk_ref[...])
        k_ref[:, :, n * HEAD_DIM:(n + 1) * HEAD_DIM] = kn.reshape(bb, tt, HEAD_DIM)
        if queries_on_lanes:
            kb_ref[0, :, n * HEAD_DIM:(n + 1) * HEAD_DIM] = kn.astype(bf16)
        o += HEAD_DIM
    v_ref[...] = p[:, o:o + N_KV * HEAD_DIM].reshape(bb, tt, N_KV * HEAD_DIM)
    if queries_on_lanes:
        for n in range(N_KV):
            vt_ref[0, 0, n] = p[:, o + n * HEAD_DIM:o + (n + 1) * HEAD_DIM].T.astype(bf16)
    o += N_KV * HEAD_DIM
    if queries_on_lanes:
        for c0 in range(0, IDX_HEADS * IDX_DIM, LANES):
            qi_ref[0, c0:c0 + LANES, :] = p[:, o + c0:o + c0 + LANES].T.astype(bf16)
    else:
        qi_ref[...] = p[:, o:o + IDX_HEADS * IDX_DIM].astype(bf16).reshape(bb, tt, IDX_HEADS * IDX_DIM)
    o += IDX_HEADS * IDX_DIM
    ki_ref[...] = p[:, o:o + IDX_DIM].reshape(bb, tt, IDX_DIM)
    if queries_on_lanes:
        kib_ref[0] = p[:, o:o + IDX_DIM].astype(bf16)
    wi_scale = IDX_HEADS ** -0.5 * IDX_DIM ** -0.5
    if queries_on_lanes:
        tail = p[:, o:o + LANES].T
        wi_ref[0] = tail[IDX_DIM:IDX_DIM + IDX_HEADS, :] * wi_scale
    else:
        wi_ref[...] = (p[:, o + IDX_DIM:o + IDX_DIM + IDX_HEADS] * wi_scale).reshape(bb, tt, IDX_HEADS)


def _attn_proj(h, w_c, g_q, g_k, queries_on_lanes):
    B, T, D = h.shape
    bb, tt = _tiles(B, T)
    assert bb == 1 or not queries_on_lanes
    kern = functools.partial(_attn_proj_kernel, bb=bb, tt=tt, queries_on_lanes=queries_on_lanes)
    widths = (N_HEADS * HEAD_DIM, N_KV * HEAD_DIM, N_KV * HEAD_DIM, IDX_HEADS * IDX_DIM, IDX_DIM, IDX_HEADS)
    dtypes = (bf16, f32, f32, bf16, f32, f32)
    flipped = (queries_on_lanes, False, False, queries_on_lanes, False, queries_on_lanes)
    col_spec = lambda n: pl.BlockSpec((1, n, tt), lambda b, t: (b, 0, t))
    out_specs = [col_spec(n) if fl else _row_spec(bb, tt, n) for n, fl in zip(widths, flipped)]
    out_shape = [jax.ShapeDtypeStruct((B, n, T) if fl else (B, T, n), dt)
                 for n, dt, fl in zip(widths, dtypes, flipped)]
    if queries_on_lanes:
        out_specs += [_row_spec(1, tt, N_KV * HEAD_DIM),
                      pl.BlockSpec((1, 1, N_KV, HEAD_DIM, tt), lambda b, t: (b, t, 0, 0, 0)),
                      _row_spec(1, tt, IDX_DIM)]
        out_shape += [jax.ShapeDtypeStruct((B, T, N_KV * HEAD_DIM), bf16),
                      jax.ShapeDtypeStruct((B, T // tt, N_KV, HEAD_DIM, tt), bf16),
                      jax.ShapeDtypeStruct((B, T, IDX_DIM), bf16)]
    return pl.pallas_call(
        kern,
        grid=(B // bb, T // tt),
        in_specs=[_row_spec(bb, tt, D), _const_spec((D, QKV_COLS)),
                  _const_spec((1, HEAD_DIM)), _const_spec((1, HEAD_DIM))],
        out_specs=out_specs,
        out_shape=out_shape,
        compiler_params=_cparams(2),
    )(h, w_c, g_q.reshape(1, HEAD_DIM), g_k.reshape(1, HEAD_DIM))


PLANE_BITS = 32
BUTTERFLY = ((16, 0x0000FFFF), (8, 0x00FF00FF), (4, 0x0F0F0F0F), (2, 0x33333333), (1, 0x55555555))


LOG2E = 1.4426950408889634
N_ALIBI = 6


def _slope(hd):
    return 2.0 ** (-8.0 * (hd + 1) / N_HEADS)


def _alibi_operands(n_keys):
    c = jnp.asarray([_slope(hd) * LOG2E for hd in range(N_HEADS)], f32)
    hi = c.astype(bf16).astype(f32)
    mid = (c - hi).astype(bf16).astype(f32)
    lo = (c - hi - mid).astype(bf16).astype(f32)
    parts = jnp.stack([hi, mid, lo], axis=1)
    q_side = jnp.concatenate([LANES * parts, parts], axis=1).astype(bf16)
    j = jnp.arange(n_keys)
    a, b = (j // LANES).astype(f32), (j % LANES).astype(f32)
    k_side = jnp.stack([a, a, a, b, b, b], axis=1).astype(bf16)
    return q_side, k_side


def _sparse_attn_kernel(q_ref, qs_ref, qi_ref, wi_ref, kt_ref, va_ref, kit_ref, o_ref,
                        key_ref, plane_ref, act_ref, qa_ref, sa_ref, sb_ref, m_ref, acc_ref,
                        *, tq, ck, S, pos0, topk):
    tpc = ck // LANES
    cpg = PLANE_BITS // tpc
    rgrp = tq // SUBLANES
    qb = pl.program_id(1)
    p_first = pos0 + qb * tq
    limit = jnp.minimum(S, ((p_first + tq - 1) // CHUNK + 1) * CHUNK)
    nkc = (limit + ck - 1) // ck
    ngrp = (nkc + cpg - 1) // cpg
    qpos = p_first + lax.broadcasted_iota(jnp.int32, (tq, 1), 0)
    qchunk = qpos // CHUNK
    lane_pos = lax.broadcasted_iota(jnp.int32, (tq, ck), 1)
    wi = wi_ref[0]
    qi = qi_ref[0]
    qih = [qi[:, hd * IDX_DIM:(hd + 1) * IDX_DIM] for hd in range(IDX_HEADS)]

    for hd in range(N_HEADS):
        n, g = divmod(hd, GQA_G)
        qa_ref[n, g * tq:(g + 1) * tq, 0:HEAD_DIM] = q_ref[0, :, hd * HEAD_DIM:(hd + 1) * HEAD_DIM]
        qa_ref[n, g * tq:(g + 1) * tq, HEAD_DIM:2 * HEAD_DIM] = qs_ref[hd]

    def score_body(c, carry):
        kic = kit_ref[0, c]
        sc = jnp.zeros((tq, ck), f32)
        for hd in range(IDX_HEADS):
            sc = sc + wi[:, hd:hd + 1] * jnp.maximum(_mm(qih[hd], kic), 0.0)
        kpos = c * ck + lane_pos
        vis = (kpos // CHUNK <= qchunk) & (kpos < S)
        bits = pltpu.bitcast(sc, jnp.int32)
        key = bits ^ ((bits >> 31) & 0x7FFFFFFF)
        key_ref[c] = jnp.where(vis, key, INT_MIN)
        return carry

    lax.fori_loop(0, nkc, score_body, 0)

    def fill_body(c, carry):
        key_ref[c] = jnp.full((tq, ck), INT_MIN, jnp.int32)
        return carry

    lax.fori_loop(nkc, ngrp * cpg, fill_body, 0)

    def xpose_body(it, carry):
        g = it // rgrp
        rows = pl.ds(pl.multiple_of((it % rgrp) * SUBLANES, SUBLANES), SUBLANES)
        w = [key_ref[g * cpg + t // tpc, rows, (t % tpc) * LANES:(t % tpc + 1) * LANES]
             for t in range(PLANE_BITS)]
        for j, msk in BUTTERFLY:
            for k in range(PLANE_BITS):
                if k & j == 0:
                    tmp = (w[k] ^ lax.shift_right_logical(w[k + j], jnp.int32(j))) & msk
                    w[k] = w[k] ^ tmp
                    w[k + j] = w[k + j] ^ (tmp << j)
        plane_ref[g, 0, rows, :] = ~w[0]
        for i in range(1, PLANE_BITS):
            plane_ref[g, i, rows, :] = w[i]
        act_ref[g, rows, :] = jnp.full((SUBLANES, LANES), -1, jnp.int32)
        return carry

    lax.fori_loop(0, ngrp * rgrp, xpose_body, 0)

    def pair_body(i, carry):
        need, thr_u = carry

        def gsum(g, parts):
            a, p1, p2 = act_ref[g], plane_ref[g, 2 * i], plane_ref[g, 2 * i + 1]
            a1, a0 = a & p1, a & ~p1
            return (parts[0] + lax.population_count(a1 & p2), parts[1] + lax.population_count(a1 & ~p2),
                    parts[2] + lax.population_count(a0 & p2))

        zero = jnp.zeros((tq, LANES), jnp.int32)
        parts = lax.fori_loop(0, ngrp, gsum, (zero, zero, zero))
        cnt = jnp.sum(jnp.concatenate(parts, axis=0), axis=1, keepdims=True)
        t1 = cnt[0:tq]
        t2 = t1 + cnt[tq:2 * tq]
        t3 = t2 + cnt[2 * tq:3 * tq]
        hi = need <= t2
        lo = (need <= t1) | (jnp.logical_not(hi) & (need <= t3))
        flip1 = jnp.where(hi, 0, -1)
        flip2 = jnp.where(lo, 0, -1)

        def gupd(g, c_):
            act_ref[g] = act_ref[g] & (plane_ref[g, 2 * i] ^ flip1) & (plane_ref[g, 2 * i + 1] ^ flip2)
            return c_

        lax.fori_loop(0, ngrp, gupd, 0)
        need = need - jnp.where(hi, jnp.where(lo, 0, t1), jnp.where(lo, t2, t3))
        bits = jnp.where(hi, 2, 0) | jnp.where(lo, 1, 0)
        return need, thr_u | lax.shift_left(bits, 30 - 2 * i)

    need, thr_u = lax.fori_loop(0, PLANE_BITS // 2, pair_body,
                                (jnp.full((tq, 1), topk, jnp.int32), jnp.zeros((tq, 1), jnp.int32)))
    thr = thr_u ^ INT_MIN
    n_eq = jnp.sum(lax.fori_loop(0, ngrp, lambda g, part: part + lax.population_count(act_ref[g]),
                                 jnp.zeros((tq, LANES), jnp.int32)), axis=1, keepdims=True)

    def count(pred):
        def body(c, part):
            hit = pred(key_ref[c], c)
            for s in range(tpc):
                part = part + hit[:, s * LANES:(s + 1) * LANES].astype(jnp.int32)
            return part
        part = lax.fori_loop(0, nkc, body, jnp.zeros((tq, LANES), jnp.int32))
        return jnp.sum(part, axis=1, keepdims=True)

    real = thr != INT_MIN
    excess = real & (n_eq > need)
    pos_bits = max(1, (ck * key_ref.shape[0] - 1).bit_length())

    def tie_search():
        def pbody(i, ans):
            cand = ans | lax.shift_left(jnp.int32(1), pos_bits - 1 - i)
            cnt = count(lambda kc, c: (kc == thr) & (c * ck + lane_pos < cand))
            return jnp.where(cnt < need, cand, ans)
        return lax.fori_loop(0, pos_bits, pbody, jnp.zeros((tq, 1), jnp.int32))

    any_excess = jnp.max(excess.astype(jnp.int32)) > 0
    tie_lim = lax.cond(any_excess, tie_search, lambda: jnp.zeros((tq, 1), jnp.int32))
    big = jnp.int32(ck * key_ref.shape[0])
    tie_lim = jnp.where(excess, tie_lim, jnp.where(real, big, -1))

    m_ref[...] = jnp.full(m_ref.shape, NEG, f32)
    acc_ref[...] = jnp.zeros(acc_ref.shape, f32)

    def qk_chunk(c, s_ref):
        for n in range(N_KV):
            s_ref[n] = _mm(qa_ref[n], kt_ref[0, c, n])

    def attn_chunk(c, s_ref, last):
        start = pl.multiple_of(c * ck, ck)
        kc = key_ref[c]
        kpos = start + lane_pos
        sel = (kc > thr) | ((kc == thr) & (kpos <= tie_lim))
        bias = jnp.where(sel, 0.0, NEG)
        if last:
            ahead = jnp.maximum(kpos - qpos, 0).astype(f32)
        for n in range(N_KV):
            ps, alphas = [], []
            for g in range(GQA_G):
                rows = slice(g * tq, (g + 1) * tq)
                sg = s_ref[n, rows, :] + bias
                if last:
                    sg = sg - (2.0 * LOG2E * _slope(n * GQA_G + g)) * ahead
                m_prev = m_ref[n, rows]
                m_new = jnp.maximum(m_prev, jnp.max(sg, axis=1, keepdims=True))
                m_ref[n, rows] = m_new
                alphas.append(jnp.exp2(m_prev - m_new))
                ps.append(jnp.exp2(sg - m_new).astype(bf16))
            pv = _mm(jnp.concatenate(ps, axis=0),
                     va_ref[0, pl.ds(start, ck), n * 2 * HEAD_DIM:(n + 1) * 2 * HEAD_DIM])
            acc_ref[n] = jnp.concatenate(alphas, axis=0) * acc_ref[n] + pv

    n_rest = nkc - 1
    qk_chunk(n_rest, sa_ref)
    qk_chunk(0, sb_ref)
    attn_chunk(n_rest, sa_ref, True)

    def attn_body(i, carry):
        qk_chunk(2 * i + 1, sa_ref)
        attn_chunk(2 * i, sb_ref, False)
        qk_chunk(2 * i + 2, sb_ref)
        attn_chunk(2 * i + 1, sa_ref, False)
        return carry

    lax.fori_loop(0, n_rest // 2, attn_body, 0)

    @pl.when(n_rest % 2 == 1)
    def _():
        attn_chunk(n_rest - 1, sb_ref, False)

    for hd in range(N_HEADS):
        n, g = divmod(hd, GQA_G)
        a = acc_ref[n, g * tq:(g + 1) * tq, :]
        o_ref[0, :, hd * HEAD_DIM:(hd + 1) * HEAD_DIM] = (a[:, :HEAD_DIM] / a[:, HEAD_DIM:]).astype(bf16)


def _sparse_attn(q, qi, wi, k_all, v_all, ki_all, pos0, tq, ck, topk):
    B, T, _ = q.shape
    S = k_all.shape[1]
    nc = -(-S // ck)
    Sp = nc * ck
    tpc = ck // LANES
    assert ck % LANES == 0 and PLANE_BITS % tpc == 0 and T % tq == 0
    cpg = PLANE_BITS // tpc
    n_grp = -(-nc // cpg)
    assert Sp <= 256 * LANES
    for qb in range(T // tq):
        p_first = pos0 + qb * tq
        limit = min(S, ((p_first + tq - 1) // CHUNK + 1) * CHUNK)
        assert p_first >= (-(-limit // ck) - 1) * ck
    pad = lambda a: jnp.pad(a.astype(bf16), ((0, 0), (0, Sp - S), (0, 0)))
    q_side, k_side = _alibi_operands(Sp)
    pos_rows = jnp.zeros((HEAD_DIM, Sp), bf16).at[:N_ALIBI].set(k_side.T)
    kt = pad(k_all).reshape(B, Sp, N_KV, HEAD_DIM).transpose(0, 2, 3, 1)
    kt = jnp.concatenate([kt, jnp.broadcast_to(pos_rows, kt.shape)], axis=2)
    kt = kt.reshape(B, N_KV, 2 * HEAD_DIM, nc, ck).transpose(0, 3, 1, 2, 4)
    va = pad(v_all).reshape(B, Sp, N_KV, HEAD_DIM)
    va = jnp.concatenate([va, jnp.ones_like(va)], axis=-1).reshape(B, Sp, N_KV * 2 * HEAD_DIM)
    kit = pad(ki_all).transpose(0, 2, 1).reshape(B, IDX_DIM, nc, ck).transpose(0, 2, 1, 3)
    qs = jnp.zeros((N_HEADS, tq, HEAD_DIM), bf16).at[:, :, :N_ALIBI].set(q_side[:, None, :])

    kern = functools.partial(_sparse_attn_kernel, tq=tq, ck=ck, S=S, pos0=pos0, topk=topk)
    hq = N_HEADS * HEAD_DIM
    once = pl.Buffered(1)
    return pl.pallas_call(
        kern,
        grid=(B, T // tq),
        in_specs=[pl.BlockSpec((1, tq, hq), lambda b, t: (b, t, 0)),
                  pl.BlockSpec((N_HEADS, tq, HEAD_DIM), lambda b, t: (0, 0, 0), pipeline_mode=once),
                  pl.BlockSpec((1, tq, IDX_HEADS * IDX_DIM), lambda b, t: (b, t, 0)),
                  pl.BlockSpec((1, tq, IDX_HEADS), lambda b, t: (b, t, 0)),
                  pl.BlockSpec((1, nc, N_KV, 2 * HEAD_DIM, ck), lambda b, t: (b, 0, 0, 0, 0), pipeline_mode=once),
                  pl.BlockSpec((1, Sp, N_KV * 2 * HEAD_DIM), lambda b, t: (b, 0, 0), pipeline_mode=once),
                  pl.BlockSpec((1, nc, IDX_DIM, ck), lambda b, t: (b, 0, 0, 0), pipeline_mode=once)],
        out_specs=pl.BlockSpec((1, tq, hq), lambda b, t: (b, t, 0)),
        out_shape=jax.ShapeDtypeStruct((B, T, hq), bf16),
        scratch_shapes=[pltpu.VMEM((n_grp * cpg, tq, ck), jnp.int32),
                        pltpu.VMEM((n_grp, PLANE_BITS, tq, LANES), jnp.int32),
                        pltpu.VMEM((n_grp, tq, LANES), jnp.int32),
                        pltpu.VMEM((N_KV, GQA_G * tq, 2 * HEAD_DIM), bf16),
                        pltpu.VMEM((N_KV, GQA_G * tq, ck), f32),
                        pltpu.VMEM((N_KV, GQA_G * tq, ck), f32),
                        pltpu.VMEM((N_KV, GQA_G * tq, 1), f32),
                        pltpu.VMEM((N_KV, GQA_G * tq, 2 * HEAD_DIM), f32)],
        compiler_params=_cparams(2),
    )(q, qs, qi, wi, kt, va, kit)


SPAN = PLANE_BITS * SUBLANES
ATTN_TQ = 256
ONES_ROWS = 16


def _sparse_attn_t_kernel(qt_ref, qst_ref, qit_ref, wit_ref, k_ref, pos_ref, vt_ref, ki_ref, ot_ref,
                          key_ref, plane_ref, act_ref, qa_ref, sa_ref, sb_ref, mxa_ref, mxb_ref, m_ref, acc_ref,
                          *, tq, ck, S, pos0, n_q, topk):
    spc = ck // SPAN
    wr = spc * SUBLANES
    ltiles = tq // LANES
    qb = pl.program_id(1)
    p_first = pos0 + qb * tq
    limit = jnp.minimum(S, ((p_first + tq - 1) // CHUNK + 1) * CHUNK)
    nkc = (limit + ck - 1) // ck
    qpos = p_first + lax.broadcasted_iota(jnp.int32, (1, tq), 1)
    qchunk = qpos // CHUNK
    row_pos = lax.broadcasted_iota(jnp.int32, (ck, tq), 0)
    wit = wit_ref[0]

    for hd in range(N_HEADS):
        n, g = divmod(hd, GQA_G)
        qa_ref[n, 0:HEAD_DIM, g * tq:(g + 1) * tq] = qt_ref[0, hd * HEAD_DIM:(hd + 1) * HEAD_DIM, :]
        qa_ref[n, HEAD_DIM:2 * HEAD_DIM, g * tq:(g + 1) * tq] = qst_ref[hd]

    n_rest = nkc - 1

    def score_chunk(c, last):
        kic = ki_ref[0, pl.ds(pl.multiple_of(c * ck, ck), ck), :]
        sc = jnp.zeros((ck, tq), f32)
        for hd in range(IDX_HEADS):
            d = _mm(kic, qit_ref[0, hd * IDX_DIM:(hd + 1) * IDX_DIM, :])
            sc = sc + wit[hd:hd + 1, :] * jnp.maximum(d, 0.0)
        bits = pltpu.bitcast(sc, jnp.int32)
        key = bits ^ ((bits >> 31) & 0x7FFFFFFF)
        if last:
            kpos = c * ck + row_pos
            key = jnp.where((kpos // CHUNK <= qchunk) & (kpos < S), key, INT_MIN)
        key_ref[c] = key
        for u in range(spc):
            for lt in range(ltiles):
                w = [key[u * SPAN + k * SUBLANES:u * SPAN + (k + 1) * SUBLANES, lt * LANES:(lt + 1) * LANES]
                     for k in range(PLANE_BITS)]
                for j, msk in BUTTERFLY:
                    for k in range(PLANE_BITS):
                        if k & j == 0:
                            tmp = (w[k] ^ lax.shift_right_logical(w[k + j], jnp.int32(j))) & msk
                            w[k] = w[k] ^ tmp
                            w[k + j] = w[k + j] ^ (tmp << j)
                rows = pl.ds(pl.multiple_of(c * wr + u * SUBLANES, SUBLANES), SUBLANES)
                lanes = slice(lt * LANES, (lt + 1) * LANES)
                plane_ref[0, rows, lanes] = jnp.full((SUBLANES, LANES), -1, jnp.int32)
                plane_ref[1, rows, lanes] = jnp.full((SUBLANES, LANES), -1, jnp.int32)
                plane_ref[2, rows, lanes] = ~w[0]
                for i in range(1, PLANE_BITS):
                    plane_ref[i + 2, rows, lanes] = w[i]
        act_ref[pl.ds(pl.multiple_of(c * wr, wr), wr), :] = jnp.full((wr, tq), -1, jnp.int32)

    def score_body(c, carry):
        score_chunk(c, False)
        return carry

    lax.fori_loop(0, n_rest, score_body, 0)
    score_chunk(n_rest, True)

    def chunk_rows(c):
        return pl.ds(pl.multiple_of(c * wr, wr), wr)

    def narrowed(c, i, flip1, flip2):
        rows = chunk_rows(c)
        return act_ref[rows, :] & (plane_ref[2 * i, rows, :] ^ flip1) & (plane_ref[2 * i + 1, rows, :] ^ flip2)

    def pair_body(i, carry):
        need, thr_u, flip1, flip2 = carry

        def csum(c, parts):
            rows = chunk_rows(c)
            a = narrowed(c, i, flip1, flip2)
            act_ref[rows, :] = a
            p1, p2 = plane_ref[2 * i + 2, rows, :], plane_ref[2 * i + 3, rows, :]
            a1 = a & p1
            a11 = a1 & p2
            return (parts[0] + lax.population_count(a11), parts[1] + lax.population_count(a1 ^ a11),
                    parts[2] + lax.population_count((a ^ a1) & p2))

        zero = jnp.zeros((wr, tq), jnp.int32)
        parts = lax.fori_loop(0, nkc, csum, (zero, zero, zero))
        t1 = jnp.sum(parts[0], axis=0, keepdims=True)
        t2 = t1 + jnp.sum(parts[1], axis=0, keepdims=True)
        t3 = t2 + jnp.sum(parts[2], axis=0, keepdims=True)
        hi = need <= t2
        lo = (need <= t1) | (jnp.logical_not(hi) & (need <= t3))
        need = need - jnp.where(hi, jnp.where(lo, 0, t1), jnp.where(lo, t2, t3))
        bits = jnp.where(hi, 2, 0) | jnp.where(lo, 1, 0)
        return need, thr_u | lax.shift_left(bits, 30 - 2 * i), jnp.where(hi, 0, -1), jnp.where(lo, 0, -1)

    zero_q = jnp.zeros((1, tq), jnp.int32)
    need, thr_u, flip1, flip2 = lax.fori_loop(0, PLANE_BITS // 2, pair_body,
                                              (jnp.full((1, tq), topk, jnp.int32), zero_q, zero_q, zero_q))
    thr = thr_u ^ INT_MIN
    n_eq = jnp.sum(lax.fori_loop(
        0, nkc, lambda c, part: part + lax.population_count(narrowed(c, PLANE_BITS // 2, flip1, flip2)),
        jnp.zeros((wr, tq), jnp.int32)), axis=0, keepdims=True)

    def count(pred):
        def body(c, part):
            return part + jnp.sum(pred(key_ref[c], c).astype(jnp.int32), axis=0, keepdims=True)
        return lax.fori_loop(0, nkc, body, jnp.zeros((1, tq), jnp.int32))

    real = thr != INT_MIN
    excess = real & (n_eq > need) & (qpos - pos0 < n_q)
    pos_bits = max(1, (ck * key_ref.shape[0] - 1).bit_length())

    def tie_search():
        def pbody(i, ans):
            cand = ans | lax.shift_left(jnp.int32(1), pos_bits - 1 - i)
            cnt = count(lambda kc, c: (kc == thr) & (c * ck + row_pos < cand))
            return jnp.where(cnt < need, cand, ans)
        return lax.fori_loop(0, pos_bits, pbody, jnp.zeros((1, tq), jnp.int32))

    any_excess = jnp.max(excess.astype(jnp.int32)) > 0
    tie_lim = lax.cond(any_excess, tie_search, lambda: jnp.zeros((1, tq), jnp.int32))
    big = jnp.int32(ck * key_ref.shape[0])
    tie_lim = jnp.where(excess, tie_lim, jnp.where(real, big, -1))
    thr_all = jnp.where(real, thr, INT_MIN + 1)

    def select_exact(kc, kpos):
        return (kc > thr) | ((kc == thr) & (kpos <= tie_lim))

    def select_all_equal(kc, kpos):
        return kc >= thr_all

    def qk_chunk(c, s_ref, mx_ref, first, select):
        rows = pl.ds(pl.multiple_of(c * ck, ck), ck)
        kpos = c * ck + row_pos
        bias = jnp.where(select(key_ref[c], kpos), 0.0, NEG)
        if first:
            ahead = jnp.maximum(kpos - qpos, 0).astype(f32)
        for n in range(N_KV):
            ka = jnp.concatenate([k_ref[0, rows, n * HEAD_DIM:(n + 1) * HEAD_DIM], pos_ref[rows, :]], axis=1)
            s = _mm(ka, qa_ref[n])
            for g in range(GQA_G):
                cols = slice(g * tq, (g + 1) * tq)
                sg = s[:, cols] + bias
                if first:
                    sg = sg - (2.0 * LOG2E * _slope(n * GQA_G + g)) * ahead
                s_ref[n, :, cols] = sg
                mx_ref[n, :, cols] = jnp.max(sg, axis=0, keepdims=True)

    def attn_chunk(c, s_ref, mx_ref, first):
        for n in range(N_KV):
            ps, alphas = [], []
            for g in range(GQA_G):
                cols = slice(g * tq, (g + 1) * tq)
                if first:
                    m_new = mx_ref[n, :, cols]
                else:
                    m_prev = m_ref[n, :, cols]
                    m_new = jnp.maximum(m_prev, mx_ref[n, :, cols])
                    alphas.append(jnp.exp2(m_prev - m_new))
                m_ref[n, :, cols] = m_new
                ps.append(jnp.exp2(s_ref[n, :, cols] - m_new).astype(bf16))
            vat = jnp.concatenate([vt_ref[0, c, n], jnp.ones((ONES_ROWS, ck), bf16)], axis=0)
            pv = _mm(vat, jnp.concatenate(ps, axis=1))
            acc_ref[n] = pv if first else jnp.concatenate(alphas, axis=1) * acc_ref[n] + pv

    def attend(select):
        qk_chunk(n_rest, sa_ref, mxa_ref, True, select)
        qk_chunk(0, sb_ref, mxb_ref, False, select)
        attn_chunk(n_rest, sa_ref, mxa_ref, True)

        def attn_body(i, carry):
            qk_chunk(2 * i + 1, sa_ref, mxa_ref, False, select)
            attn_chunk(2 * i, sb_ref, mxb_ref, False)
            qk_chunk(2 * i + 2, sb_ref, mxb_ref, False, select)
            attn_chunk(2 * i + 1, sa_ref, mxa_ref, False)
            return carry

        lax.fori_loop(0, n_rest // 2, attn_body, 0)

        @pl.when(n_rest % 2 == 1)
        def _():
            attn_chunk(n_rest - 1, sb_ref, mxb_ref, False)

    pl.when(any_excess)(lambda: attend(select_exact))
    pl.when(jnp.logical_not(any_excess))(lambda: attend(select_all_equal))

    for hd in range(N_HEADS):
        n, g = divmod(hd, GQA_G)
        a = acc_ref[n, :, g * tq:(g + 1) * tq]
        ot_ref[0, hd * HEAD_DIM:(hd + 1) * HEAD_DIM, :] = (a[:HEAD_DIM] / a[HEAD_DIM:HEAD_DIM + 1]).astype(bf16)


def _key_operands_t(k_all, v_all, ki_all, ck):
    B, S, _ = k_all.shape
    nc = -(-S // ck)
    pad = lambda a: jnp.pad(a.astype(bf16), ((0, 0), (0, nc * ck - S), (0, 0)))
    vt = pad(v_all).reshape(B, nc, ck, N_KV, HEAD_DIM).transpose(0, 1, 3, 4, 2)
    return pad(k_all), vt, pad(ki_all)


def _sparse_attn_t(qt, qit, wit, kb, vt, kib, S, pos0, n_q, tq, ck, topk):
    B, _, T = qt.shape
    nc = vt.shape[1]
    Sp = nc * ck
    assert kb.shape[1] == Sp and kib.shape[1] == Sp and Sp - ck < S <= Sp
    assert ck % SPAN == 0 and tq % LANES == 0 and T % tq == 0
    assert Sp <= 256 * LANES
    for qb in range(T // tq):
        p_first = pos0 + qb * tq
        limit = min(S, ((p_first + tq - 1) // CHUNK + 1) * CHUNK)
        assert p_first >= (-(-limit // ck) - 1) * ck
    q_side, k_side = _alibi_operands(Sp)
    pos_cols = jnp.zeros((Sp, HEAD_DIM), bf16).at[:, :N_ALIBI].set(k_side)
    qst = jnp.zeros((N_HEADS, HEAD_DIM, tq), bf16).at[:, :N_ALIBI, :].set(q_side[:, :, None])

    kern = functools.partial(_sparse_attn_t_kernel, tq=tq, ck=ck, S=S, pos0=pos0, n_q=n_q, topk=topk)
    hq = N_HEADS * HEAD_DIM
    once = pl.Buffered(1)
    wr = ck // SPAN * SUBLANES
    return pl.pallas_call(
        kern,
        grid=(B, T // tq),
        in_specs=[pl.BlockSpec((1, hq, tq), lambda b, t: (b, 0, t)),
                  pl.BlockSpec((N_HEADS, HEAD_DIM, tq), lambda b, t: (0, 0, 0), pipeline_mode=once),
                  pl.BlockSpec((1, IDX_HEADS * IDX_DIM, tq), lambda b, t: (b, 0, t)),
                  pl.BlockSpec((1, IDX_HEADS, tq), lambda b, t: (b, 0, t)),
                  pl.BlockSpec((1, Sp, N_KV * HEAD_DIM), lambda b, t: (b, 0, 0), pipeline_mode=once),
                  pl.BlockSpec((Sp, HEAD_DIM), lambda b, t: (0, 0), pipeline_mode=once),
                  pl.BlockSpec((1, nc, N_KV, HEAD_DIM, ck), lambda b, t: (b, 0, 0, 0, 0), pipeline_mode=once),
                  pl.BlockSpec((1, Sp, IDX_DIM), lambda b, t: (b, 0, 0), pipeline_mode=once)],
        out_specs=pl.BlockSpec((1, hq, tq), lambda b, t: (b, 0, t)),
        out_shape=jax.ShapeDtypeStruct((B, hq, T), bf16),
        scratch_shapes=[pltpu.VMEM((nc, ck, tq), jnp.int32),
                        pltpu.VMEM((PLANE_BITS + 2, nc * wr, tq), jnp.int32),
                        pltpu.VMEM((nc * wr, tq), jnp.int32),
                        pltpu.VMEM((N_KV, 2 * HEAD_DIM, GQA_G * tq), bf16),
                        pltpu.VMEM((N_KV, ck, GQA_G * tq), f32),
                        pltpu.VMEM((N_KV, ck, GQA_G * tq), f32),
                        pltpu.VMEM((N_KV, 1, GQA_G * tq), f32),
                        pltpu.VMEM((N_KV, 1, GQA_G * tq), f32),
                        pltpu.VMEM((N_KV, 1, GQA_G * tq), f32),
                        pltpu.VMEM((N_KV, HEAD_DIM + ONES_ROWS, GQA_G * tq), f32)],
        compiler_params=_cparams(2),
    )(qt, qst, qit, wit, kb, pos_cols, vt, kib)


def _merge_kernel(x_ref, h_ref, ya_ref, yb_ref, o_ref, g1_ref, wgl_ref, wpc_ref, wo_ref,
                  gn2_ref, sc2_ref, sh2_ref, x1_ref, h2_ref, *, bb, tt, o_on_lanes):
    D = D_MODEL
    M = bb * tt
    h = h_ref[...].reshape(M, D)
    if o_on_lanes:
        yc = lax.dot_general(o_ref[0], wpc_ref[...], (((0,), (0,)), ((), ())), preferred_element_type=f32)
    else:
        yc = _mm(o_ref[...].reshape(M, D), wpc_ref[...])
    m = _sigmoid(_mm(h, wgl_ref[:, 0:D])) * ya_ref[...].reshape(M, D).astype(f32)
    m = m + _sigmoid(_mm(h, wgl_ref[:, D:2 * D])) * yb_ref[...].reshape(M, D).astype(f32)
    m = m + _sigmoid(_mm(h, wgl_ref[:, 2 * D:3 * D])) * yc
    x1 = x_ref[...] + g1_ref[...] * _mm(m.astype(bf16), wo_ref[...]).reshape(bb, tt, D)
    x1_ref[...] = x1
    h2 = _rms(x1, gn2_ref[...]) * (1.0 + sc2_ref[...]) + sh2_ref[...]
    h2_ref[...] = h2.astype(bf16)


def _merge(x, h, ya, yb, o, g1, wgl, wpc, wo, gn2, sc2, sh2, o_on_lanes):
    B, T, D = x.shape
    bb, tt = _tiles(B, T)
    assert bb == 1 or not o_on_lanes
    kern = functools.partial(_merge_kernel, bb=bb, tt=tt, o_on_lanes=o_on_lanes)
    row = _row_spec(bb, tt, D)
    vec = _batch_spec(bb, 1, D)
    o_spec = pl.BlockSpec((1, D, tt), lambda b, t: (b, 0, t)) if o_on_lanes else row
    return pl.pallas_call(
        kern,
        grid=(B // bb, T // tt),
        in_specs=[row, row, row, row, o_spec, vec, _const_spec((D, 3 * D)), _const_spec((D, D)),
                  _const_spec((D, D)), _const_spec((1, 1, D)), vec, vec],
        out_specs=[row, row],
        out_shape=[jax.ShapeDtypeStruct((B, T, D), f32), jax.ShapeDtypeStruct((B, T, D), bf16)],
        compiler_params=_cparams(2),
    )(x, h, ya, yb, o, g1, wgl, wpc, wo, gn2.reshape(1, 1, D), sc2, sh2)


FF_TILE = 1024


def _mlp_kernel(x1_ref, h2_ref, g2_ref, w1_ref, b1_ref, w2_ref, b2_ref, x2_ref, *, bb, tt):
    D = D_MODEL
    M = bb * tt
    h2 = h2_ref[...].reshape(M, D)
    acc = jnp.zeros((M, D), f32)
    for c0 in range(0, D_FF, FF_TILE):
        f = jnp.maximum(_mm(h2, w1_ref[:, c0:c0 + FF_TILE]) + b1_ref[:, c0:c0 + FF_TILE], 0.0)
        acc = acc + _mm((f * f).astype(bf16), w2_ref[c0:c0 + FF_TILE, :])
    f = (acc + b2_ref[...]).reshape(bb, tt, D)
    x2_ref[...] = x1_ref[...] + g2_ref[...] * f


def _mlp(x1, h2, g2, w1, b1, w2, b2):
    B, T, D = x1.shape
    bb, tt = _tiles(B, T)
    kern = functools.partial(_mlp_kernel, bb=bb, tt=tt)
    row = _row_spec(bb, tt, D)
    return pl.pallas_call(
        kern,
        grid=(B // bb, T // tt),
        in_specs=[row, row, _batch_spec(bb, 1, D), _const_spec((D, D_FF)), _const_spec((1, D_FF)),
                  _const_spec((D_FF, D)), _const_spec((1, D))],
        out_specs=row,
        out_shape=jax.ShapeDtypeStruct((B, T, D), f32),
        compiler_params=_cparams(2),
    )(x1, h2, g2, w1, b1.reshape(1, D_FF), w2, b2.reshape(1, D))


def _split_w_in(w_in):
    D = D_MODEL
    w = w_in.astype(bf16)
    o_b = 2 * D
    o_q = 4 * D
    o_gl = o_q + N_HEADS * HEAD_DIM + 2 * N_KV * HEAD_DIM + IDX_HEADS * IDX_DIM + IDX_DIM + IDX_HEADS
    n_c = o_gl - o_q
    w_c = jnp.pad(w[:, o_q:o_gl], ((0, 0), (0, QKV_COLS - n_c)))
    return dict(wa1=w[:, 0:D], wa2=w[:, D:2 * D], wbu=w[:, o_b:o_b + D], wbv=w[:, o_b + D:o_b + 2 * D],
                w_c=w_c, wgl=w[:, o_gl:o_gl + 3 * D])


def _layer(x, mod, cache, W, pos0):
    B, T, D = x.shape
    sh1, sc1, g1, sh2, sc2, g2 = mod
    win = W['w_in']
    h = _prenorm(x, W['g_norm1'], sc1, sh1)

    hist = jnp.zeros((B, HIST, D), f32) if cache is None else cache[3]
    ya, new_conv = _branch_conv(h, hist, win['wa1'], win['wa2'], W['w_dw'], W['b_dw'],
                                W['ln_c_g'], W['ln_c_b'], W['w_pa'])

    L = GMLP_CHUNK if cache is None else T
    res = _branch_gmlp(h, win['wbu'], win['wbv'], W['ln_v_g'], W['ln_v_b'], W['w_s'], W['b_s'],
                       W['w_pb'], L, cache is not None)
    yb = res[0]
    vg = res[1] if cache is not None else None

    on_lanes = T % ATTN_TQ == 0
    q, k, v, qi, ki, wi, *key_ops = _attn_proj(h, win['w_c'], W['g_q'], W['g_k'], on_lanes)
    if cache is None:
        k_all, v_all, ki_all = k, v, ki
    else:
        k_all = jnp.concatenate([cache[0].reshape(B, -1, N_KV * HEAD_DIM), k], axis=1)
        v_all = jnp.concatenate([cache[1].reshape(B, -1, N_KV * HEAD_DIM), v], axis=1)
        ki_all = jnp.concatenate([cache[2], ki], axis=1)
    S = k_all.shape[1]
    topk = min(TOPK_MAX, S // 4)
    ck = ROW_TILE
    if cache is not None or not on_lanes:
        key_ops = _key_operands_t(k_all, v_all, ki_all, ck)
    if on_lanes:
        o = _sparse_attn_t(q, qi, wi, *key_ops, S, pos0, T, ATTN_TQ, ck, topk)
    else:
        to_lanes = lambda a: jnp.pad(a.transpose(0, 2, 1), ((0, 0), (0, 0), (0, -T % LANES)))
        ot = _sparse_attn_t(to_lanes(q), to_lanes(qi), to_lanes(wi), *key_ops, S, pos0, T, LANES, ck, topk)
        o = ot[:, :, :T].transpose(0, 2, 1)

    x1, h2 = _merge(x, h, ya, yb, o, g1, win['wgl'], W['w_pc'], W['w_o'], W['g_norm2'], sc2, sh2, on_lanes)
    x2 = _mlp(x1, h2, g2, W['w_1'], W['b_1'], W['w_2'], W['b_2'])
    return x2, (k.reshape(B, T, N_KV, HEAD_DIM), v.reshape(B, T, N_KV, HEAD_DIM), ki, new_conv, vg)


def kernel(x_prompt, x_sample, c_prompt, c_sample, cache_k, cache_v, cache_kidx, cache_conv, w_ada, b_ada, g_norm1, w_in, w_dw, b_dw, ln_c_g, ln_c_b, w_pa, ln_v_g, ln_v_b, w_s, b_s, w_pb, g_q, g_k, w_pc, w_o, g_norm2, w_1, b_1, w_2, b_2):
    depth = w_in.shape[0]
    D = D_MODEL
    Bp, Bs = x_prompt.shape[0], x_sample.shape[0]
    past = cache_k.shape[2]
    c_all = jnp.concatenate([c_prompt, c_sample], axis=0)
    n_c = c_all.shape[0]
    n_pad = -(-n_c // SUBLANES) * SUBLANES
    mod_all = _ada(jnp.pad(c_all, ((0, n_pad - n_c), (0, 0))), w_ada.astype(bf16), b_ada)

    def mods(l, lo, n):
        m = mod_all[l, lo:lo + n]
        return [m[:, None, i * D:(i + 1) * D] for i in range(6)]

    hp, hs = x_prompt, x_sample
    outs_p, outs_s = [], []
    for l in range(depth):
        W = dict(g_norm1=g_norm1[l], w_in=_split_w_in(w_in[l]), w_dw=w_dw[l], b_dw=b_dw[l],
                 ln_c_g=ln_c_g[l], ln_c_b=ln_c_b[l], w_pa=w_pa[l].astype(bf16), ln_v_g=ln_v_g[l],
                 ln_v_b=ln_v_b[l], w_s=w_s[l], b_s=b_s[l], w_pb=w_pb[l].astype(bf16), g_q=g_q[l],
                 g_k=g_k[l], w_pc=w_pc[l].astype(bf16), w_o=w_o[l].astype(bf16), g_norm2=g_norm2[l],
                 w_1=w_1[l].astype(bf16), b_1=b_1[l], w_2=w_2[l].astype(bf16), b_2=b_2[l])
        hp, op = _layer(hp, mods(l, 0, Bp), None, W, 0)
        hs, os_ = _layer(hs, mods(l, Bp, Bs), (cache_k[l], cache_v[l], cache_kidx[l], cache_conv[l]), W, past)
        outs_p.append(op)
        outs_s.append(os_)
    stack = lambda outs, i: jnp.stack([o[i] for o in outs])
    return (hp, hs,
            stack(outs_p, 0), stack(outs_p, 1), stack(outs_p, 2), stack(outs_p, 3),
            stack(outs_s, 0), stack(outs_s, 1), stack(outs_s, 2), stack(outs_s, 3), stack(outs_s, 4))
```

```python
import functools

import jax
import jax.numpy as jnp
from jax import lax
from jax.experimental import pallas as pl
from jax.experimental.pallas import tpu as pltpu

D_MODEL = 1024
CHUNK = 64
CONV_W = 31
HIST = CONV_W - 1
GMLP_CHUNK = 128
GMLP_GROUPS = 8
GMLP_GW = D_MODEL // GMLP_GROUPS
N_HEADS = 8
HEAD_DIM = 128
N_KV = 2
GQA_G = N_HEADS // N_KV
IDX_HEADS = 8
IDX_DIM = 64
TOPK_MAX = 256
Q_BLOCK = 128
D_FF = 4 * D_MODEL
EPS = 1e-6

LANES = 128
SUBLANES = 8
HALO = 32
ROW_TILE = 512
VMEM_LIMIT = 56 * 1024 * 1024
INT_MIN = -2 ** 31
NEG = -1e30

f32 = jnp.float32
bf16 = jnp.bfloat16


def _cparams(n_axes):
    return pltpu.CompilerParams(dimension_semantics=("arbitrary",) * n_axes,
                                vmem_limit_bytes=VMEM_LIMIT)


def _tiles(B, T):
    if T >= ROW_TILE:
        assert T % ROW_TILE == 0
        return 1, ROW_TILE
    assert ROW_TILE % T == 0 and B % (ROW_TILE // T) == 0
    return ROW_TILE // T, T


def _row_spec(bb, tt, n):
    return pl.BlockSpec((bb, tt, n), lambda b, t: (b, t, 0))


def _batch_spec(bb, r, n):
    return pl.BlockSpec((bb, r, n), lambda b, t: (b, 0, 0))


def _const_spec(shape):
    nd = len(shape)
    return pl.BlockSpec(shape, lambda b, t: (0,) * nd, pipeline_mode=pl.Buffered(1))


def _mm(a, w):
    return jnp.dot(a, w, preferred_element_type=f32)


def _sigmoid(x):
    return 1.0 / (1.0 + jnp.exp(-x))


def _silu(x):
    return x * _sigmoid(x)


def _gelu_tanh(x):
    return 0.5 * x * (1.0 + jnp.tanh(0.7978845608028654 * (x + 0.044715 * (x * x * x))))


def _rms(x, g):
    return x * lax.rsqrt(jnp.mean(x * x, axis=-1, keepdims=True) + EPS) * g


def _ln(x, g, b):
    mu = jnp.mean(x, axis=-1, keepdims=True)
    xc = x - mu
    var = jnp.mean(xc * xc, axis=-1, keepdims=True)
    return xc * lax.rsqrt(var + EPS) * g + b


def _ada_kernel(c_ref, w_ref, b_ref, o_ref):
    c = c_ref[...]
    o_ref[0] = _mm(_silu(c).astype(bf16), w_ref[0]) + b_ref[0]


def _ada(c_all, w_ada, b_ada):
    depth, d, n = w_ada.shape
    bp = c_all.shape[0]
    tn = 1024
    return pl.pallas_call(
        _ada_kernel,
        grid=(depth, n // tn),
        in_specs=[pl.BlockSpec((bp, d), lambda l, j: (0, 0)),
                  pl.BlockSpec((1, d, tn), lambda l, j: (l, 0, j)),
                  pl.BlockSpec((1, 1, tn), lambda l, j: (l, 0, j))],
        out_specs=pl.BlockSpec((1, bp, tn), lambda l, j: (l, 0, j)),
        out_shape=jax.ShapeDtypeStruct((depth, bp, n), f32),
        compiler_params=_cparams(2),
    )(c_all, w_ada, b_ada.reshape(depth, 1, n))


def _prenorm_kernel(x_ref, g_ref, sc_ref, sh_ref, h_ref):
    h = _rms(x_ref[...], g_ref[...]) * (1.0 + sc_ref[...]) + sh_ref[...]
    h_ref[...] = h.astype(bf16)


def _prenorm(x, g, sc, sh):
    B, T, D = x.shape
    bb, tt = _tiles(B, T)
    return pl.pallas_call(
        _prenorm_kernel,
        grid=(B // bb, T // tt),
        in_specs=[_row_spec(bb, tt, D), _const_spec((1, 1, D)),
                  _batch_spec(bb, 1, D), _batch_spec(bb, 1, D)],
        out_specs=_row_spec(bb, tt, D),
        out_shape=jax.ShapeDtypeStruct((B, T, D), bf16),
        compiler_params=_cparams(2),
    )(x, g.reshape(1, 1, D), sc, sh)


CONV_ROWS = 64
CONV_COLS = 256


def _conv_kernel(h_ref, hist_ref, wa1_ref, wa2_ref, wdw_ref, bdw_ref, lng_ref, lnb_ref, wpa_ref,
                 ya_ref, newc_ref, apad_ref, conv_ref, shift_ref, *, bb, tt):
    D = D_MODEL
    t = pl.program_id(1)

    @pl.when(t == 0)
    def _():
        apad_ref[:, 0:HALO - HIST, :] = jnp.zeros((bb, HALO - HIST, D), f32)
        apad_ref[:, HALO - HIST:HALO, :] = hist_ref[...]

    h = h_ref[...].reshape(bb * tt, D)
    a = _mm(h, wa1_ref[...]) * _sigmoid(_mm(h, wa2_ref[...]))
    apad_ref[:, HALO:HALO + tt, :] = a.reshape(bb, tt, D)

    base = HALO - HIST
    n_sh = HALO + tt - SUBLANES
    for b in range(bb):
        for c0 in range(0, D, CONV_COLS):
            cols = slice(c0, c0 + CONV_COLS)
            for s in range(1, SUBLANES):
                shift_ref[s - 1, 0:n_sh, :] = apad_ref[b, s:s + n_sh, cols]
            taps = [jnp.broadcast_to(wdw_ref[j:j + 1, cols], (SUBLANES, CONV_COLS)) for j in range(CONV_W)]
            for r0 in range(0, tt, CONV_ROWS):
                acc = jnp.zeros((CONV_ROWS // SUBLANES, SUBLANES, CONV_COLS), f32)
                for j in range(CONV_W):
                    q8, s = divmod(base + j, SUBLANES)
                    lo = r0 + SUBLANES * q8
                    src = (apad_ref[b, lo:lo + CONV_ROWS, cols] if s == 0
                           else shift_ref[s - 1, lo:lo + CONV_ROWS, :])
                    acc = acc + taps[j][None] * src.reshape(CONV_ROWS // SUBLANES, SUBLANES, CONV_COLS)
                acc = acc.reshape(CONV_ROWS, CONV_COLS)
                conv_ref[b * tt + r0:b * tt + r0 + CONV_ROWS, cols] = acc

    tail = apad_ref[:, HALO + tt - HIST:HALO + tt, :]
    newc_ref[...] = tail
    apad_ref[:, HALO - HIST:HALO, :] = tail

    y = conv_ref[...] + bdw_ref[...]
    z = _silu(_ln(y, lng_ref[...], lnb_ref[...]))
    ya_ref[...] = _mm(z.astype(bf16), wpa_ref[...]).astype(bf16).reshape(bb, tt, D)


def _branch_conv(h, hist, wa1, wa2, wdw, bdw, lng, lnb, wpa):
    B, T, D = h.shape
    bb, tt = _tiles(B, T)
    kern = functools.partial(_conv_kernel, bb=bb, tt=tt)
    return pl.pallas_call(
        kern,
        grid=(B // bb, T // tt),
        in_specs=[_row_spec(bb, tt, D), _batch_spec(bb, HIST, D),
                  _const_spec((D, D)), _const_spec((D, D)), _const_spec((CONV_W, D)),
                  _const_spec((1, D)), _const_spec((1, D)), _const_spec((1, D)), _const_spec((D, D))],
        out_specs=[_row_spec(bb, tt, D), _batch_spec(bb, HIST, D)],
        out_shape=[jax.ShapeDtypeStruct((B, T, D), bf16), jax.ShapeDtypeStruct((B, HIST, D), f32)],
        scratch_shapes=[pltpu.VMEM((bb, HALO + tt, D), f32), pltpu.VMEM((bb * tt, D), f32),
                        pltpu.VMEM((SUBLANES - 1, HALO + tt, CONV_COLS), f32)],
        compiler_params=_cparams(2),
    )(h, hist, wa1, wa2, wdw, bdw.reshape(1, D), lng.reshape(1, D), lnb.reshape(1, D), wpa)


def _gmlp_kernel(h_ref, wbu_ref, wbv_ref, lng_ref, lnb_ref, wm_ref, bs_ref, wpb_ref,
                 yb_ref, *rest, bb, tt, L, emit_v):
    D = D_MODEL
    if emit_v:
        vg_ref, s_ref = rest
    else:
        (s_ref,) = rest
    M = bb * tt
    h = h_ref[...].reshape(M, D)
    u = _gelu_tanh(_mm(h, wbu_ref[...]))
    vg = _ln(_gelu_tanh(_mm(h, wbv_ref[...])), lng_ref[...], lnb_ref[...])
    if emit_v:
        vg_ref[...] = vg.reshape(bb, tt, D)
    vgb = vg.astype(bf16)
    for c in range(M // L):
        for g in range(GMLP_GROUPS):
            blk = vgb[c * L:(c + 1) * L, g * GMLP_GW:(g + 1) * GMLP_GW]
            s_ref[c * L:(c + 1) * L, g * GMLP_GW:(g + 1) * GMLP_GW] = _mm(wm_ref[g], blk) + bs_ref[:, g * GMLP_GW:(g + 1) * GMLP_GW]
    yb_ref[...] = _mm((u * s_ref[...]).astype(bf16), wpb_ref[...]).astype(bf16).reshape(bb, tt, D)


def _branch_gmlp(h, wbu, wbv, lng, lnb, w_s, b_s, wpb, L, emit_v):
    B, T, D = h.shape
    bb, tt = _tiles(B, T)
    assert tt % L == 0
    tril = jnp.tril(jnp.ones((L, L), dtype=bool))
    wm = jnp.where(tril[None], w_s[:, :L, :L], 0).astype(bf16)
    bs = jnp.repeat(jnp.swapaxes(b_s[:, :L], 0, 1), GMLP_GW, axis=1)
    kern = functools.partial(_gmlp_kernel, bb=bb, tt=tt, L=L, emit_v=emit_v)
    out_specs = [_row_spec(bb, tt, D)]
    out_shape = [jax.ShapeDtypeStruct((B, T, D), bf16)]
    if emit_v:
        out_specs.append(_row_spec(bb, tt, D))
        out_shape.append(jax.ShapeDtypeStruct((B, T, D), f32))
    return pl.pallas_call(
        kern,
        grid=(B // bb, T // tt),
        in_specs=[_row_spec(bb, tt, D), _const_spec((D, D)), _const_spec((D, D)),
                  _const_spec((1, D)), _const_spec((1, D)), _const_spec((GMLP_GROUPS, L, L)),
                  _const_spec((L, D)), _const_spec((D, D))],
        out_specs=out_specs,
        out_shape=out_shape,
        scratch_shapes=[pltpu.VMEM((bb * tt, D), f32)],
        compiler_params=_cparams(2),
    )(h, wbu, wbv, lng.reshape(1, D), lnb.reshape(1, D), wm, bs, wpb)


QKV_COLS = N_HEADS * HEAD_DIM + 2 * N_KV * HEAD_DIM + IDX_HEADS * IDX_DIM + LANES


def _attn_proj_kernel(h_ref, w_ref, gq_ref, gk_ref, q_ref, k_ref, v_ref, qi_ref, ki_ref, wi_ref, *rest,
                      bb, tt, queries_on_lanes):
    D = D_MODEL
    M = bb * tt
    p = _mm(h_ref[...].reshape(M, D), w_ref[...])
    if queries_on_lanes:
        kb_ref, vt_ref, kib_ref = rest
    o = 0
    for hd in range(N_HEADS):
        qh = _rms(p[:, o:o + HEAD_DIM], gq_ref[...]) * (HEAD_DIM ** -0.5 * LOG2E)
        if queries_on_lanes:
            q_ref[0, hd * HEAD_DIM:(hd + 1) * HEAD_DIM, :] = qh.T.astype(bf16)
        else:
            q_ref[:, :, hd * HEAD_DIM:(hd + 1) * HEAD_DIM] = qh.astype(bf16).reshape(bb, tt, HEAD_DIM)
        o += HEAD_DIM
    for n in range(N_KV):
        kn = _rms(p[:, o:o + HEAD_DIM], gk_ref[...])
        k_ref[:, :, n * HEAD_DIM:(n + 1) * HEAD_DIM] = kn.reshape(bb, tt, HEAD_DIM)
        if queries_on_lanes:
            kb_ref[0, :, n * HEAD_DIM:(n + 1) * HEAD_DIM] = kn.astype(bf16)
        o += HEAD_DIM
    v_ref[...] = p[:, o:o + N_KV * HEAD_DIM].reshape(bb, tt, N_KV * HEAD_DIM)
    if queries_on_lanes:
        for n in range(N_KV):
            vt_ref[0, 0, n] = p[:, o + n * HEAD_DIM:o + (n + 1) * HEAD_DIM].T.astype(bf16)
    o += N_KV * HEAD_DIM
    if queries_on_lanes:
        for c0 in range(0, IDX_HEADS * IDX_DIM, LANES):
            qi_ref[0, c0:c0 + LANES, :] = p[:, o + c0:o + c0 + LANES].T.astype(bf16)
    else:
        qi_ref[...] = p[:, o:o + IDX_HEADS * IDX_DIM].astype(bf16).reshape(bb, tt, IDX_HEADS * IDX_DIM)
    o += IDX_HEADS * IDX_DIM
    ki_ref[...] = p[:, o:o + IDX_DIM].reshape(bb, tt, IDX_DIM)
    if queries_on_lanes:
        kib_ref[0] = p[:, o:o + IDX_DIM].astype(bf16)
    wi_scale = IDX_HEADS ** -0.5 * IDX_DIM ** -0.5
    if queries_on_lanes:
        tail = p[:, o:o + LANES].T
        wi_ref[0] = tail[IDX_DIM:IDX_DIM + IDX_HEADS, :] * wi_scale
    else:
        wi_ref[...] = (p[:, o + IDX_DIM:o + IDX_DIM + IDX_HEADS] * wi_scale).reshape(bb, tt, IDX_HEADS)


def _attn_proj(h, w_c, g_q, g_k, queries_on_lanes):
    B, T, D = h.shape
    bb, tt = _tiles(B, T)
    assert bb == 1 or not queries_on_lanes
    kern = functools.partial(_attn_proj_kernel, bb=bb, tt=tt, queries_on_lanes=queries_on_lanes)
    widths = (N_HEADS * HEAD_DIM, N_KV * HEAD_DIM, N_KV * HEAD_DIM, IDX_HEADS * IDX_DIM, IDX_DIM, IDX_HEADS)
    dtypes = (bf16, f32, f32, bf16, f32, f32)
    flipped = (queries_on_lanes, False, False, queries_on_lanes, False, queries_on_lanes)
    col_spec = lambda n: pl.BlockSpec((1, n, tt), lambda b, t: (b, 0, t))
    out_specs = [col_spec(n) if fl else _row_spec(bb, tt, n) for n, fl in zip(widths, flipped)]
    out_shape = [jax.ShapeDtypeStruct((B, n, T) if fl else (B, T, n), dt)
                 for n, dt, fl in zip(widths, dtypes, flipped)]
    if queries_on_lanes:
        out_specs += [_row_spec(1, tt, N_KV * HEAD_DIM),
                      pl.BlockSpec((1, 1, N_KV, HEAD_DIM, tt), lambda b, t: (b, t, 0, 0, 0)),
                      _row_spec(1, tt, IDX_DIM)]
        out_shape += [jax.ShapeDtypeStruct((B, T, N_KV * HEAD_DIM), bf16),
                      jax.ShapeDtypeStruct((B, T // tt, N_KV, HEAD_DIM, tt), bf16),
                      jax.ShapeDtypeStruct((B, T, IDX_DIM), bf16)]
    return pl.pallas_call(
        kern,
        grid=(B // bb, T // tt),
        in_specs=[_row_spec(bb, tt, D), _const_spec((D, QKV_COLS)),
                  _const_spec((1, HEAD_DIM)), _const_spec((1, HEAD_DIM))],
        out_specs=out_specs,
        out_shape=out_shape,
        compiler_params=_cparams(2),
    )(h, w_c, g_q.reshape(1, HEAD_DIM), g_k.reshape(1, HEAD_DIM))


PLANE_BITS = 32
BUTTERFLY = ((16, 0x0000FFFF), (8, 0x00FF00FF), (4, 0x0F0F0F0F), (2, 0x33333333), (1, 0x55555555))


LOG2E = 1.4426950408889634
N_ALIBI = 6


def _slope(hd):
    return 2.0 ** (-8.0 * (hd + 1) / N_HEADS)


def _alibi_operands(n_keys):
    c = jnp.asarray([_slope(hd) * LOG2E for hd in range(N_HEADS)], f32)
    hi = c.astype(bf16).astype(f32)
    mid = (c - hi).astype(bf16).astype(f32)
    lo = (c - hi - mid).astype(bf16).astype(f32)
    parts = jnp.stack([hi, mid, lo], axis=1)
    q_side = jnp.concatenate([LANES * parts, parts], axis=1).astype(bf16)
    j = jnp.arange(n_keys)
    a, b = (j // LANES).astype(f32), (j % LANES).astype(f32)
    k_side = jnp.stack([a, a, a, b, b, b], axis=1).astype(bf16)
    return q_side, k_side


SPAN = PLANE_BITS * SUBLANES
ATTN_TQ = 256
ONES_ROWS = 16


def _sparse_attn_t_kernel(qt_ref, qst_ref, qit_ref, wit_ref, k_ref, pos_ref, vt_ref, ki_ref, ot_ref,
                          key_ref, plane_ref, act_ref, qa_ref, sa_ref, sb_ref, mxa_ref, mxb_ref, m_ref, acc_ref,
                          *, tq, ck, S, pos0, n_q, topk):
    spc = ck // SPAN
    wr = spc * SUBLANES
    ltiles = tq // LANES
    qb = pl.program_id(1)
    p_first = pos0 + qb * tq
    limit = jnp.minimum(S, ((p_first + tq - 1) // CHUNK + 1) * CHUNK)
    nkc = (limit + ck - 1) // ck
    qpos = p_first + lax.broadcasted_iota(jnp.int32, (1, tq), 1)
    qchunk = qpos // CHUNK
    row_pos = lax.broadcasted_iota(jnp.int32, (ck, tq), 0)
    wit = wit_ref[0]

    for hd in range(N_HEADS):
        n, g = divmod(hd, GQA_G)
        qa_ref[n, 0:HEAD_DIM, g * tq:(g + 1) * tq] = qt_ref[0, hd * HEAD_DIM:(hd + 1) * HEAD_DIM, :]
        qa_ref[n, HEAD_DIM:2 * HEAD_DIM, g * tq:(g + 1) * tq] = qst_ref[hd]

    n_rest = nkc - 1

    def score_chunk(c, last):
        kic = ki_ref[0, pl.ds(pl.multiple_of(c * ck, ck), ck), :]
        sc = jnp.zeros((ck, tq), f32)
        for hd in range(IDX_HEADS):
            d = _mm(kic, qit_ref[0, hd * IDX_DIM:(hd + 1) * IDX_DIM, :])
            sc = sc + wit[hd:hd + 1, :] * jnp.maximum(d, 0.0)
        bits = pltpu.bitcast(sc, jnp.int32)
        key = bits ^ ((bits >> 31) & 0x7FFFFFFF)
        if last:
            kpos = c * ck + row_pos
            key = jnp.where((kpos // CHUNK <= qchunk) & (kpos < S), key, INT_MIN)
        key_ref[c] = key
        for u in range(spc):
            for lt in range(ltiles):
                w = [key[u * SPAN + k * SUBLANES:u * SPAN + (k + 1) * SUBLANES, lt * LANES:(lt + 1) * LANES]
                     for k in range(PLANE_BITS)]
                for j, msk in BUTTERFLY:
                    for k in range(PLANE_BITS):
                        if k & j == 0:
                            tmp = (w[k] ^ lax.shift_right_logical(w[k + j], jnp.int32(j))) & msk
                            w[k] = w[k] ^ tmp
                            w[k + j] = w[k + j] ^ (tmp << j)
                rows = pl.ds(pl.multiple_of(c * wr + u * SUBLANES, SUBLANES), SUBLANES)
                lanes = slice(lt * LANES, (lt + 1) * LANES)
                plane_ref[0, rows, lanes] = jnp.full((SUBLANES, LANES), -1, jnp.int32)
                plane_ref[1, rows, lanes] = jnp.full((SUBLANES, LANES), -1, jnp.int32)
                plane_ref[2, rows, lanes] = ~w[0]
                for i in range(1, PLANE_BITS):
                    plane_ref[i + 2, rows, lanes] = w[i]
        act_ref[pl.ds(pl.multiple_of(c * wr, wr), wr), :] = jnp.full((wr, tq), -1, jnp.int32)

    def score_body(c, carry):
        score_chunk(c, False)
        return carry

    lax.fori_loop(0, n_rest, score_body, 0)
    score_chunk(n_rest, True)

    def chunk_rows(c):
        return pl.ds(pl.multiple_of(c * wr, wr), wr)

    def narrowed(c, i, flip1, flip2):
        rows = chunk_rows(c)
        return act_ref[rows, :] & (plane_ref[2 * i, rows, :] ^ flip1) & (plane_ref[2 * i + 1, rows, :] ^ flip2)

    def pair_body(i, carry):
        need, thr_u, flip1, flip2 = carry

        def csum(c, parts):
            rows = chunk_rows(c)
            a = narrowed(c, i, flip1, flip2)
            act_ref[rows, :] = a
            p1, p2 = plane_ref[2 * i + 2, rows, :], plane_ref[2 * i + 3, rows, :]
            a1 = a & p1
            a11 = a1 & p2
            return (parts[0] + lax.population_count(a11), parts[1] + lax.population_count(a1 ^ a11),
                    parts[2] + lax.population_count((a ^ a1) & p2))

        zero = jnp.zeros((wr, tq), jnp.int32)
        parts = lax.fori_loop(0, nkc, csum, (zero, zero, zero))
        t1 = jnp.sum(parts[0], axis=0, keepdims=True)
        t2 = t1 + jnp.sum(parts[1], axis=0, keepdims=True)
        t3 = t2 + jnp.sum(parts[2], axis=0, keepdims=True)
        hi = need <= t2
        lo = (need <= t1) | (jnp.logical_not(hi) & (need <= t3))
        need = need - jnp.where(hi, jnp.where(lo, 0, t1), jnp.where(lo, t2, t3))
        bits = jnp.where(hi, 2, 0) | jnp.where(lo, 1, 0)
        return need, thr_u | lax.shift_left(bits, 30 - 2 * i), jnp.where(hi, 0, -1), jnp.where(lo, 0, -1)

    zero_q = jnp.zeros((1, tq), jnp.int32)
    need, thr_u, flip1, flip2 = lax.fori_loop(0, PLANE_BITS // 2, pair_body,
                                              (jnp.full((1, tq), topk, jnp.int32), zero_q, zero_q, zero_q))
    thr = thr_u ^ INT_MIN
    n_eq = jnp.sum(lax.fori_loop(
        0, nkc, lambda c, part: part + lax.population_count(narrowed(c, PLANE_BITS // 2, flip1, flip2)),
        jnp.zeros((wr, tq), jnp.int32)), axis=0, keepdims=True)

    def count(pred):
        def body(c, part):
            return part + jnp.sum(pred(key_ref[c], c).astype(jnp.int32), axis=0, keepdims=True)
        return lax.fori_loop(0, nkc, body, jnp.zeros((1, tq), jnp.int32))

    real = thr != INT_MIN
    excess = real & (n_eq > need) & (qpos - pos0 < n_q)
    pos_bits = max(1, (ck * key_ref.shape[0] - 1).bit_length())

    def tie_search():
        def pbody(i, ans):
            cand = ans | lax.shift_left(jnp.int32(1), pos_bits - 1 - i)
            cnt = count(lambda kc, c: (kc == thr) & (c * ck + row_pos < cand))
            return jnp.where(cnt < need, cand, ans)
        return lax.fori_loop(0, pos_bits, pbody, jnp.zeros((1, tq), jnp.int32))

    any_excess = jnp.max(excess.astype(jnp.int32)) > 0
    tie_lim = lax.cond(any_excess, tie_search, lambda: jnp.zeros((1, tq), jnp.int32))
    big = jnp.int32(ck * key_ref.shape[0])
    tie_lim = jnp.where(excess, tie_lim, jnp.where(real, big, -1))
    thr_all = jnp.where(real, thr, INT_MIN + 1)

    def select_exact(kc, kpos):
        return (kc > thr) | ((kc == thr) & (kpos <= tie_lim))

    def select_all_equal(kc, kpos):
        return kc >= thr_all

    def qk_chunk(c, s_ref, mx_ref, first, select):
        rows = pl.ds(pl.multiple_of(c * ck, ck), ck)
        kpos = c * ck + row_pos
        bias = jnp.where(select(key_ref[c], kpos), 0.0, NEG)
        if first:
            ahead = jnp.maximum(kpos - qpos, 0).astype(f32)
        for n in range(N_KV):
            ka = jnp.concatenate([k_ref[0, rows, n * HEAD_DIM:(n + 1) * HEAD_DIM], pos_ref[rows, :]], axis=1)
            s = _mm(ka, qa_ref[n])
            for g in range(GQA_G):
                cols = slice(g * tq, (g + 1) * tq)
                sg = s[:, cols] + bias
                if first:
                    sg = sg - (2.0 * LOG2E * _slope(n * GQA_G + g)) * ahead
                s_ref[n, :, cols] = sg
                mx_ref[n, :, cols] = jnp.max(sg, axis=0, keepdims=True)

    def attn_chunk(c, s_ref, mx_ref, first):
        for n in range(N_KV):
            ps, alphas = [], []
            for g in range(GQA_G):
                cols = slice(g * tq, (g + 1) * tq)
                if first:
                    m_new = mx_ref[n, :, cols]
                else:
                    m_prev = m_ref[n, :, cols]
                    m_new = jnp.maximum(m_prev, mx_ref[n, :, cols])
                    alphas.append(jnp.exp2(m_prev - m_new))
                m_ref[n, :, cols] = m_new
                ps.append(jnp.exp2(s_ref[n, :, cols] - m_new).astype(bf16))
            vat = jnp.concatenate([vt_ref[0, c, n], jnp.ones((ONES_ROWS, ck), bf16)], axis=0)
            pv = _mm(vat, jnp.concatenate(ps, axis=1))
            acc_ref[n] = pv if first else jnp.concatenate(alphas, axis=1) * acc_ref[n] + pv

    def attend(select):
        qk_chunk(n_rest, sa_ref, mxa_ref, True, select)
        qk_chunk(0, sb_ref, mxb_ref, False, select)
        attn_chunk(n_rest, sa_ref, mxa_ref, True)

        def attn_body(i, carry):
            qk_chunk(2 * i + 1, sa_ref, mxa_ref, False, select)
            attn_chunk(2 * i, sb_ref, mxb_ref, False)
            qk_chunk(2 * i + 2, sb_ref, mxb_ref, False, select)
            attn_chunk(2 * i + 1, sa_ref, mxa_ref, False)
            return carry

        lax.fori_loop(0, n_rest // 2, attn_body, 0)

        @pl.when(n_rest % 2 == 1)
        def _():
            attn_chunk(n_rest - 1, sb_ref, mxb_ref, False)

    pl.when(any_excess)(lambda: attend(select_exact))
    pl.when(jnp.logical_not(any_excess))(lambda: attend(select_all_equal))

    for hd in range(N_HEADS):
        n, g = divmod(hd, GQA_G)
        a = acc_ref[n, :, g * tq:(g + 1) * tq]
        ot_ref[0, hd * HEAD_DIM:(hd + 1) * HEAD_DIM, :] = (a[:HEAD_DIM] / a[HEAD_DIM:HEAD_DIM + 1]).astype(bf16)


def _key_operands_t(k_all, v_all, ki_all, ck):
    B, S, _ = k_all.shape
    nc = -(-S // ck)
    pad = lambda a: jnp.pad(a.astype(bf16), ((0, 0), (0, nc * ck - S), (0, 0)))
    vt = pad(v_all).reshape(B, nc, ck, N_KV, HEAD_DIM).transpose(0, 1, 3, 4, 2)
    return pad(k_all), vt, pad(ki_all)


def _sparse_attn_t(qt, qit, wit, kb, vt, kib, S, pos0, n_q, tq, ck, topk):
    B, _, T = qt.shape
    nc = vt.shape[1]
    Sp = nc * ck
    assert kb.shape[1] == Sp and kib.shape[1] == Sp and Sp - ck < S <= Sp
    assert ck % SPAN == 0 and tq % LANES == 0 and T % tq == 0
    assert Sp <= 256 * LANES
    for qb in range(T // tq):
        p_first = pos0 + qb * tq
        limit = min(S, ((p_first + tq - 1) // CHUNK + 1) * CHUNK)
        assert p_first >= (-(-limit // ck) - 1) * ck
    q_side, k_side = _alibi_operands(Sp)
    pos_cols = jnp.zeros((Sp, HEAD_DIM), bf16).at[:, :N_ALIBI].set(k_side)
    qst = jnp.zeros((N_HEADS, HEAD_DIM, tq), bf16).at[:, :N_ALIBI, :].set(q_side[:, :, None])

    kern = functools.partial(_sparse_attn_t_kernel, tq=tq, ck=ck, S=S, pos0=pos0, n_q=n_q, topk=topk)
    hq = N_HEADS * HEAD_DIM
    once = pl.Buffered(1)
    wr = ck // SPAN * SUBLANES
    return pl.pallas_call(
        kern,
        grid=(B, T // tq),
        in_specs=[pl.BlockSpec((1, hq, tq), lambda b, t: (b, 0, t)),
                  pl.BlockSpec((N_HEADS, HEAD_DIM, tq), lambda b, t: (0, 0, 0), pipeline_mode=once),
                  pl.BlockSpec((1, IDX_HEADS * IDX_DIM, tq), lambda b, t: (b, 0, t)),
                  pl.BlockSpec((1, IDX_HEADS, tq), lambda b, t: (b, 0, t)),
                  pl.BlockSpec((1, Sp, N_KV * HEAD_DIM), lambda b, t: (b, 0, 0), pipeline_mode=once),
                  pl.BlockSpec((Sp, HEAD_DIM), lambda b, t: (0, 0), pipeline_mode=once),
                  pl.BlockSpec((1, nc, N_KV, HEAD_DIM, ck), lambda b, t: (b, 0, 0, 0, 0), pipeline_mode=once),
                  pl.BlockSpec((1, Sp, IDX_DIM), lambda b, t: (b, 0, 0), pipeline_mode=once)],
        out_specs=pl.BlockSpec((1, hq, tq), lambda b, t: (b, 0, t)),
        out_shape=jax.ShapeDtypeStruct((B, hq, T), bf16),
        scratch_shapes=[pltpu.VMEM((nc, ck, tq), jnp.int32),
                        pltpu.VMEM((PLANE_BITS + 2, nc * wr, tq), jnp.int32),
                        pltpu.VMEM((nc * wr, tq), jnp.int32),
                        pltpu.VMEM((N_KV, 2 * HEAD_DIM, GQA_G * tq), bf16),
                        pltpu.VMEM((N_KV, ck, GQA_G * tq), f32),
                        pltpu.VMEM((N_KV, ck, GQA_G * tq), f32),
                        pltpu.VMEM((N_KV, 1, GQA_G * tq), f32),
                        pltpu.VMEM((N_KV, 1, GQA_G * tq), f32),
                        pltpu.VMEM((N_KV, 1, GQA_G * tq), f32),
                        pltpu.VMEM((N_KV, HEAD_DIM + ONES_ROWS, GQA_G * tq), f32)],
        compiler_params=_cparams(2),
    )(qt, qst, qit, wit, kb, pos_cols, vt, kib)


def _merge_kernel(x_ref, h_ref, ya_ref, yb_ref, o_ref, g1_ref, wgl_ref, wpc_ref, wo_ref,
                  gn2_ref, sc2_ref, sh2_ref, x1_ref, h2_ref, *, bb, tt, o_on_lanes):
    D = D_MODEL
    M = bb * tt
    h = h_ref[...].reshape(M, D)
    if o_on_lanes:
        yc = lax.dot_general(o_ref[0], wpc_ref[...], (((0,), (0,)), ((), ())), preferred_element_type=f32)
    else:
        yc = _mm(o_ref[...].reshape(M, D), wpc_ref[...])
    m = _sigmoid(_mm(h, wgl_ref[:, 0:D])) * ya_ref[...].reshape(M, D).astype(f32)
    m = m + _sigmoid(_mm(h, wgl_ref[:, D:2 * D])) * yb_ref[...].reshape(M, D).astype(f32)
    m = m + _sigmoid(_mm(h, wgl_ref[:, 2 * D:3 * D])) * yc
    x1 = x_ref[...] + g1_ref[...] * _mm(m.astype(bf16), wo_ref[...]).reshape(bb, tt, D)
    x1_ref[...] = x1
    h2 = _rms(x1, gn2_ref[...]) * (1.0 + sc2_ref[...]) + sh2_ref[...]
    h2_ref[...] = h2.astype(bf16)


def _merge(x, h, ya, yb, o, g1, wgl, wpc, wo, gn2, sc2, sh2, o_on_lanes):
    B, T, D = x.shape
    bb, tt = _tiles(B, T)
    assert bb == 1 or not o_on_lanes
    kern = functools.partial(_merge_kernel, bb=bb, tt=tt, o_on_lanes=o_on_lanes)
    row = _row_spec(bb, tt, D)
    vec = _batch_spec(bb, 1, D)
    o_spec = pl.BlockSpec((1, D, tt), lambda b, t: (b, 0, t)) if o_on_lanes else row
    return pl.pallas_call(
        kern,
        grid=(B // bb, T // tt),
        in_specs=[row, row, row, row, o_spec, vec, _const_spec((D, 3 * D)), _const_spec((D, D)),
                  _const_spec((D, D)), _const_spec((1, 1, D)), vec, vec],
        out_specs=[row, row],
        out_shape=[jax.ShapeDtypeStruct((B, T, D), f32), jax.ShapeDtypeStruct((B, T, D), bf16)],
        compiler_params=_cparams(2),
    )(x, h, ya, yb, o, g1, wgl, wpc, wo, gn2.reshape(1, 1, D), sc2, sh2)


FF_TILE = 1024


def _mlp_kernel(x1_ref, h2_ref, g2_ref, w1_ref, b1_ref, w2_ref, b2_ref, x2_ref, *, bb, tt):
    D = D_MODEL
    M = bb * tt
    h2 = h2_ref[...].reshape(M, D)
    acc = jnp.zeros((M, D), f32)
    for c0 in range(0, D_FF, FF_TILE):
        f = jnp.maximum(_mm(h2, w1_ref[:, c0:c0 + FF_TILE]) + b1_ref[:, c0:c0 + FF_TILE], 0.0)
        acc = acc + _mm((f * f).astype(bf16), w2_ref[c0:c0 + FF_TILE, :])
    f = (acc + b2_ref[...]).reshape(bb, tt, D)
    x2_ref[...] = x1_ref[...] + g2_ref[...] * f


def _mlp(x1, h2, g2, w1, b1, w2, b2):
    B, T, D = x1.shape
    bb, tt = _tiles(B, T)
    kern = functools.partial(_mlp_kernel, bb=bb, tt=tt)
    row = _row_spec(bb, tt, D)
    return pl.pallas_call(
        kern,
        grid=(B // bb, T // tt),
        in_specs=[row, row, _batch_spec(bb, 1, D), _const_spec((D, D_FF)), _const_spec((1, D_FF)),
                  _const_spec((D_FF, D)), _const_spec((1, D))],
        out_specs=row,
        out_shape=jax.ShapeDtypeStruct((B, T, D), f32),
        compiler_params=_cparams(2),
    )(x1, h2, g2, w1, b1.reshape(1, D_FF), w2, b2.reshape(1, D))


def _split_w_in(w_in):
    D = D_MODEL
    w = w_in.astype(bf16)
    o_b = 2 * D
    o_q = 4 * D
    o_gl = o_q + N_HEADS * HEAD_DIM + 2 * N_KV * HEAD_DIM + IDX_HEADS * IDX_DIM + IDX_DIM + IDX_HEADS
    n_c = o_gl - o_q
    w_c = jnp.pad(w[:, o_q:o_gl], ((0, 0), (0, QKV_COLS - n_c)))
    return dict(wa1=w[:, 0:D], wa2=w[:, D:2 * D], wbu=w[:, o_b:o_b + D], wbv=w[:, o_b + D:o_b + 2 * D],
                w_c=w_c, wgl=w[:, o_gl:o_gl + 3 * D])


def _layer(x, mod, cache, W, pos0):
    B, T, D = x.shape
    sh1, sc1, g1, sh2, sc2, g2 = mod
    win = W['w_in']
    h = _prenorm(x, W['g_norm1'], sc1, sh1)

    hist = jnp.zeros((B, HIST, D), f32) if cache is None else cache[3]
    ya, new_conv = _branch_conv(h, hist, win['wa1'], win['wa2'], W['w_dw'], W['b_dw'],
                                W['ln_c_g'], W['ln_c_b'], W['w_pa'])

    L = GMLP_CHUNK if cache is None else T
    res = _branch_gmlp(h, win['wbu'], win['wbv'], W['ln_v_g'], W['ln_v_b'], W['w_s'], W['b_s'],
                       W['w_pb'], L, cache is not None)
    yb = res[0]
    vg = res[1] if cache is not None else None

    on_lanes = T % ATTN_TQ == 0
    q, k, v, qi, ki, wi, *key_ops = _attn_proj(h, win['w_c'], W['g_q'], W['g_k'], on_lanes)
    if cache is None:
        k_all, v_all, ki_all = k, v, ki
    else:
        k_all = jnp.concatenate([cache[0].reshape(B, -1, N_KV * HEAD_DIM), k], axis=1)
        v_all = jnp.concatenate([cache[1].reshape(B, -1, N_KV * HEAD_DIM), v], axis=1)
        ki_all = jnp.concatenate([cache[2], ki], axis=1)
    S = k_all.shape[1]
    topk = min(TOPK_MAX, S // 4)
    ck = ROW_TILE
    if cache is not None or not on_lanes:
        key_ops = _key_operands_t(k_all, v_all, ki_all, ck)
    if on_lanes:
        o = _sparse_attn_t(q, qi, wi, *key_ops, S, pos0, T, ATTN_TQ, ck, topk)
    else:
        to_lanes = lambda a: jnp.pad(a.transpose(0, 2, 1), ((0, 0), (0, 0), (0, -T % LANES)))
        ot = _sparse_attn_t(to_lanes(q), to_lanes(qi), to_lanes(wi), *key_ops, S, pos0, T, LANES, ck, topk)
        o = ot[:, :, :T].transpose(0, 2, 1)

    x1, h2 = _merge(x, h, ya, yb, o, g1, win['wgl'], W['w_pc'], W['w_o'], W['g_norm2'], sc2, sh2, on_lanes)
    x2 = _mlp(x1, h2, g2, W['w_1'], W['b_1'], W['w_2'], W['b_2'])
    return x2, (k.reshape(B, T, N_KV, HEAD_DIM), v.reshape(B, T, N_KV, HEAD_DIM), ki, new_conv, vg)


def kernel(x_prompt, x_sample, c_prompt, c_sample, cache_k, cache_v, cache_kidx, cache_conv, w_ada, b_ada, g_norm1, w_in, w_dw, b_dw, ln_c_g, ln_c_b, w_pa, ln_v_g, ln_v_b, w_s, b_s, w_pb, g_q, g_k, w_pc, w_o, g_norm2, w_1, b_1, w_2, b_2):
    depth = w_in.shape[0]
    D = D_MODEL
    Bp, Bs = x_prompt.shape[0], x_sample.shape[0]
    past = cache_k.shape[2]
    c_all = jnp.concatenate([c_prompt, c_sample], axis=0)
    n_c = c_all.shape[0]
    n_pad = -(-n_c // SUBLANES) * SUBLANES
    mod_all = _ada(jnp.pad(c_all, ((0, n_pad - n_c), (0, 0))), w_ada.astype(bf16), b_ada)

    def mods(l, lo, n):
        m = mod_all[l, lo:lo + n]
        return [m[:, None, i * D:(i + 1) * D] for i in range(6)]

    hp, hs = x_prompt, x_sample
    outs_p, outs_s = [], []
    for l in range(depth):
        W = dict(g_norm1=g_norm1[l], w_in=_split_w_in(w_in[l]), w_dw=w_dw[l], b_dw=b_dw[l],
                 ln_c_g=ln_c_g[l], ln_c_b=ln_c_b[l], w_pa=w_pa[l].astype(bf16), ln_v_g=ln_v_g[l],
                 ln_v_b=ln_v_b[l], w_s=w_s[l], b_s=b_s[l], w_pb=w_pb[l].astype(bf16), g_q=g_q[l],
                 g_k=g_k[l], w_pc=w_pc[l].astype(bf16), w_o=w_o[l].astype(bf16), g_norm2=g_norm2[l],
                 w_1=w_1[l].astype(bf16), b_1=b_1[l], w_2=w_2[l].astype(bf16), b_2=b_2[l])
        hp, op = _layer(hp, mods(l, 0, Bp), None, W, 0)
        hs, os_ = _layer(hs, mods(l, Bp, Bs), (cache_k[l], cache_v[l], cache_kidx[l], cache_conv[l]), W, past)
        outs_p.append(op)
        outs_s.append(os_)
    stack = lambda outs, i: jnp.stack([o[i] for o in outs])
    return (hp, hs,
            stack(outs_p, 0), stack(outs_p, 1), stack(outs_p, 2), stack(outs_p, 3),
            stack(outs_s, 0), stack(outs_s, 1), stack(outs_s, 2), stack(outs_s, 3), stack(outs_s, 4))
```
